```python
import math
import numpy as np
import jax
import jax.numpy as jnp
from jax import lax

D_MODEL = 2048
BATCH = 8
SEQ = 4096
DEPTH = 2
DEC_BATCH = 2
DEC_SEQ = 8192
PAST_LEN = 128

GRID_W = 64
EPS = 1e-6

NA_HEADS = 8
NA_HEAD_DIM = 128
NA_WIDTH = NA_HEADS * NA_HEAD_DIM
NA_KR_MAX = 8
NA_KC = 16
NA_QBLK = 16
NA_KBLK = 32

ML_HEADS = 4
ML_HEAD_DIM = 256
ML_WIDTH = ML_HEADS * ML_HEAD_DIM
ML_CHUNK = 64
ML_CONV_W = 3

SG_GROUPS = 8
SG_CHUNK = 128
SG_WIDTH = 1024
SG_GROUP_DIM = SG_WIDTH // SG_GROUPS

D_FF = 4 * D_MODEL
N_BRANCH = 3

OFF_A = 0
OFF_B = OFF_A + 3 * NA_WIDTH
OFF_BG = OFF_B + 4 * ML_WIDTH
OFF_C = OFF_BG + 4 * ML_HEADS
OFF_G = OFF_C + 2 * SG_WIDTH
D_IN = OFF_G + N_BRANCH * D_MODEL

kernel_name = "hybrid_bidir_natten_mlstm_sgmlp_encoder"


def rmsnorm(x, g):
    xf = x.astype(jnp.float32)
    y = xf * lax.rsqrt(jnp.mean(xf * xf, axis=-1, keepdims=True) + EPS)
    return (y * g.astype(jnp.float32)).astype(x.dtype)


def centred_dwconv(x, w):
    K = w.shape[0]
    p = K // 2
    T = x.shape[1]
    xp = jnp.pad(x, ((0, 0), (p, p), (0, 0)))
    return sum(xp[:, j:j + T] * w[j] for j in range(K))


def neighbourhood_attention(q, k, v, rpb):
    B, T, H, Dh = q.shape
    rows = T // GRID_W
    kr = min(NA_KR_MAX, rows)
    n_cb = GRID_W // NA_QBLK
    q_cols = np.arange(GRID_W).reshape(n_cb, NA_QBLK)
    kc0 = np.clip(q_cols[:, 0] - NA_KC // 2, 0, GRID_W - NA_KBLK)
    k_cols = kc0[:, None] + np.arange(NA_KBLK)[None, :]
    cs = np.clip(q_cols - NA_KC // 2, 0, GRID_W - NA_KC)
    col_ok = (k_cols[:, None, :] >= cs[:, :, None]) & (k_cols[:, None, :] < cs[:, :, None] + NA_KC)
    dc_idx = np.clip(k_cols[:, None, :] - q_cols[:, :, None] + NA_KC - 1, 0, 2 * NA_KC - 2)
    col_bias = rpb[:, :, dc_idx]
    mask = jnp.asarray(col_ok)[None, None, :, :, None, :]
    qg = q.reshape(B, rows, n_cb, NA_QBLK, H, Dh)
    kg = k.reshape(B, rows, GRID_W, H, Dh)
    vg = v.reshape(B, rows, GRID_W, H, Dh)
    scale = Dh ** -0.5

    def one_row(r):
        rs = jnp.clip(r - kr // 2, 0, rows - kr)
        q_r = lax.dynamic_index_in_dim(qg, r, axis=1, keepdims=False)
        k_r = lax.dynamic_slice_in_dim(kg, rs, kr, axis=1)[:, :, k_cols]
        v_r = lax.dynamic_slice_in_dim(vg, rs, kr, axis=1)[:, :, k_cols]
        bias = col_bias[:, rs - r + NA_KR_MAX - 1 + jnp.arange(kr)]
        bias = bias.transpose(0, 2, 3, 1, 4)[None].astype(jnp.float32)
        s = jnp.einsum('bnqhd,brnkhd->bhnqrk', q_r, k_r).astype(jnp.float32) * scale + bias
        s = jnp.where(mask, s, -jnp.inf)
        p = jax.nn.softmax(s.reshape(B, H, n_cb, NA_QBLK, kr * NA_KBLK), axis=-1)
        p = p.reshape(B, H, n_cb, NA_QBLK, kr, NA_KBLK).astype(v.dtype)
        return jnp.einsum('bhnqrk,brnkhd->bnqhd', p, v_r)

    out = lax.map(one_row, jnp.arange(rows))
    return jnp.moveaxis(out, 0, 1).reshape(B, T, H * Dh)


def mlstm_chunkwise(q, k, v, i_pre, f_pre):
    B, H, T, Dh = q.shape
    L = ML_CHUNK
    nc = T // L

    def to_chunks(a):
        a = a.reshape(B, H, nc, L, *a.shape[3:])
        return jnp.moveaxis(a, 2, 0)

    logf = jax.nn.log_sigmoid(f_pre)
    xs = (to_chunks(q), to_chunks(k), to_chunks(v), to_chunks(i_pre), to_chunks(logf))
    lower = jnp.tril(jnp.ones((L, L), dtype=bool))

    def step(carry, inp):
        C, n, m = carry
        qc, kc, vc, ic, lfc = inp
        b = jnp.cumsum(lfc, axis=-1)
        dlog = jnp.where(lower, b[..., :, None] - b[..., None, :] + ic[..., None, :], -jnp.inf)
        inter = b + m[..., None]
        m_row = jnp.maximum(inter, jnp.max(dlog, axis=-1))
        s = jnp.einsum('bhqd,bhkd->bhqk', qc, kc) * jnp.exp(dlog - m_row[..., None])
        w_inter = jnp.exp(inter - m_row)
        num = jnp.einsum('bhqk,bhkd->bhqd', s, vc) + w_inter[..., None] * jnp.einsum('bhqd,bhde->bhqe', qc, C)
        nq = jnp.sum(s, axis=-1) + w_inter * jnp.einsum('bhqd,bhd->bhq', qc, n)
        h = num / jnp.maximum(jnp.abs(nq), jnp.exp(-m_row))[..., None]
        b_last = b[..., -1]
        wlog = b_last[..., None] - b + ic
        m_new = jnp.maximum(b_last + m, jnp.max(wlog, axis=-1))
        w = jnp.exp(wlog - m_new[..., None])
        decay = jnp.exp(b_last + m - m_new)
        C_new = decay[..., None, None] * C + jnp.einsum('bhs,bhsd,bhse->bhde', w, kc, vc)
        n_new = decay[..., None] * n + jnp.einsum('bhs,bhsd->bhd', w, kc)
        return (C_new, n_new, m_new), h

    init = (jnp.zeros((B, H, Dh, Dh), jnp.float32), jnp.zeros((B, H, Dh), jnp.float32),
            jnp.zeros((B, H), jnp.float32))
    _, hs = lax.scan(step, init, xs)
    return jnp.moveaxis(hs, 0, 2).reshape(B, H, T, Dh)


def mlstm_branch(qkvo, gate_pre, conv_w, norm_g):
    B, T, _ = qkvo.shape
    qk = jax.nn.silu(centred_dwconv(qkvo[..., :2 * ML_WIDTH], conv_w))
    v = qkvo[..., 2 * ML_WIDTH:3 * ML_WIDTH]
    o = qkvo[..., 3 * ML_WIDTH:]

    def heads(a):
        return a.reshape(B, T, ML_HEADS, ML_HEAD_DIM).transpose(0, 2, 1, 3).astype(jnp.float32)

    qh = heads(qk[..., :ML_WIDTH]) * (ML_HEAD_DIM ** -0.5)
    kh = heads(qk[..., ML_WIDTH:])
    vh = heads(v)
    g = gate_pre.astype(jnp.float32).reshape(B, T, 4, ML_HEADS).transpose(2, 0, 3, 1)
    h_fwd = mlstm_chunkwise(qh, kh, vh, g[0], g[1])
    flip = lambda a: jnp.flip(a, axis=2)
    h_bwd = flip(mlstm_chunkwise(flip(qh), flip(kh), flip(vh), flip(g[2]), flip(g[3])))
    h = h_fwd + h_bwd
    h = h * lax.rsqrt(jnp.mean(h * h, axis=-1, keepdims=True) + EPS)
    h = h.transpose(0, 2, 1, 3).reshape(B, T, ML_WIDTH) * norm_g.astype(jnp.float32)
    return (h * jax.nn.sigmoid(o.astype(jnp.float32))).astype(qkvo.dtype)


def spatial_gating_branch(uv, norm_g, w_s, b_s):
    B, T, _ = uv.shape
    z = jax.nn.gelu(uv)
    u = z[..., :SG_WIDTH]
    v = rmsnorm(z[..., SG_WIDTH:], norm_g)
    vg = v.reshape(B, T // SG_CHUNK, SG_CHUNK, SG_GROUPS, SG_GROUP_DIM)
    mixed = jnp.einsum('gpq,bcqgd->bcpgd', w_s, vg) + b_s.T[None, None, :, :, None]
    return u * mixed.reshape(B, T, SG_WIDTH)


def trunk_layer(x, pre_mix_g, post_mix_g, pre_mlp_g, post_mlp_g, w_in, b_gate, conv_w, na_rpb,
                ml_norm_g, sg_norm_g, w_s, b_s, w_a, w_b, w_c, w_out, w_up, w_down):
    B, T, _ = x.shape
    h = rmsnorm(x, pre_mix_g)
    proj = lambda lo, hi: h @ w_in[:, lo:hi]
    qkv_a = proj(OFF_A, OFF_B).reshape(B, T, 3, NA_HEADS, NA_HEAD_DIM)
    y_a = neighbourhood_attention(qkv_a[:, :, 0], qkv_a[:, :, 1], qkv_a[:, :, 2], na_rpb)
    y_b = mlstm_branch(proj(OFF_B, OFF_BG), proj(OFF_BG, OFF_C) + b_gate, conv_w, ml_norm_g)
    y_c = spatial_gating_branch(proj(OFF_C, OFF_G), sg_norm_g, w_s, b_s)
    gates = jax.nn.sigmoid(proj(OFF_G, D_IN).reshape(B, T, N_BRANCH, D_MODEL))
    merged = gates[:, :, 0] * (y_a @ w_a) + gates[:, :, 1] * (y_b @ w_b) + gates[:, :, 2] * (y_c @ w_c)
    x = x + rmsnorm(merged @ w_out, post_mix_g)
    h = rmsnorm(x, pre_mlp_g)
    f = jnp.square(jax.nn.relu(h @ w_up)) @ w_down
    return x + rmsnorm(f, post_mlp_g)


def setup_inputs(seed: int = 0) -> dict:
    key = jax.random.key(seed)
    ks = jax.random.split(key, 24)
    L = DEPTH

    def nrm(k, shape, scale):
        return jax.random.normal(k, shape, jnp.float32) * scale

    i_bias = nrm(ks[8], (L, 2, ML_HEADS), 0.1)
    f_bias = jnp.linspace(3.0, 6.0, ML_HEADS, dtype=jnp.float32)[None, None, :] + nrm(ks[9], (L, 2, ML_HEADS), 0.1)
    b_gate = jnp.stack([i_bias[:, 0], f_bias[:, 0], i_bias[:, 1], f_bias[:, 1]], axis=1).reshape(L, 4 * ML_HEADS)
    return {
        "x_prompt": nrm(ks[0], (BATCH, SEQ, D_MODEL), 1.0),
        "x_sample": nrm(ks[1], (DEC_BATCH, DEC_SEQ, D_MODEL), 1.0),
        "pre_mix_g": 1.0 + nrm(ks[2], (L, D_MODEL), 0.1),
        "post_mix_g": 1.0 + nrm(ks[3], (L, D_MODEL), 0.1),
        "pre_mlp_g": 1.0 + nrm(ks[4], (L, D_MODEL), 0.1),
        "post_mlp_g": 1.0 + nrm(ks[5], (L, D_MODEL), 0.1),
        "w_in": nrm(ks[6], (L, D_MODEL, D_IN), D_MODEL ** -0.5),
        "b_gate": b_gate,
        "conv_w": nrm(ks[7], (L, ML_CONV_W, 2 * ML_WIDTH), ML_CONV_W ** -0.5),
        "na_rpb": nrm(ks[10], (L, NA_HEADS, 2 * NA_KR_MAX - 1, 2 * NA_KC - 1), 0.1),
        "ml_norm_g": 1.0 + nrm(ks[11], (L, ML_WIDTH), 0.1),
        "sg_norm_g": 1.0 + nrm(ks[12], (L, SG_WIDTH), 0.1),
        "w_s": nrm(ks[13], (L, SG_GROUPS, SG_CHUNK, SG_CHUNK), SG_CHUNK ** -0.5),
        "b_s": 1.0 + nrm(ks[14], (L, SG_GROUPS, SG_CHUNK), 0.1),
        "w_a": nrm(ks[15], (L, NA_WIDTH, D_MODEL), NA_WIDTH ** -0.5),
        "w_b": nrm(ks[16], (L, ML_WIDTH, D_MODEL), ML_WIDTH ** -0.5),
        "w_c": nrm(ks[17], (L, SG_WIDTH, D_MODEL), SG_WIDTH ** -0.5),
        "w_out": nrm(ks[18], (L, D_MODEL, D_MODEL), D_MODEL ** -0.5),
        "w_up": nrm(ks[19], (L, D_MODEL, D_FF), D_MODEL ** -0.5),
        "w_down": nrm(ks[20], (L, D_FF, D_MODEL), D_FF ** -0.5),
    }


def reference(x_prompt, x_sample, pre_mix_g, post_mix_g, pre_mlp_g, post_mlp_g, w_in, b_gate, conv_w,
              na_rpb, ml_norm_g, sg_norm_g, w_s, b_s, w_a, w_b, w_c, w_out, w_up, w_down):
    def run(x):
        for l in range(DEPTH):
            x = trunk_layer(x, pre_mix_g[l], post_mix_g[l], pre_mlp_g[l], post_mlp_g[l], w_in[l], b_gate[l],
                            conv_w[l], na_rpb[l], ml_norm_g[l], sg_norm_g[l], w_s[l], b_s[l], w_a[l], w_b[l],
                            w_c[l], w_out[l], w_up[l], w_down[l])
        return x

    y_prompt = run(x_prompt)
    y_sample = run(x_sample)
    return (y_prompt, y_sample)
```

```python
import functools

import numpy as np
import jax
import jax.numpy as jnp
from jax import lax
from jax.experimental import pallas as pl
from jax.experimental.pallas import tpu as pltpu

F32 = jnp.float32
BF16 = jnp.bfloat16

D_MODEL = 2048
GRID_W = 64
EPS = 1e-6

NA_HEADS = 8
NA_HEAD_DIM = 128
NA_WIDTH = NA_HEADS * NA_HEAD_DIM
NA_KR = 8
NA_KC = 16

ML_HEADS = 4
ML_HEAD_DIM = 256
ML_WIDTH = ML_HEADS * ML_HEAD_DIM
ML_CHUNK = 256

SG_GROUPS = 8
SG_CHUNK = 128
SG_WIDTH = 1024
SG_GROUP_DIM = SG_WIDTH // SG_GROUPS

D_FF = 4 * D_MODEL
N_BRANCH = 3

OFF_A = 0
OFF_B = OFF_A + 3 * NA_WIDTH
OFF_BG = OFF_B + 4 * ML_WIDTH
OFF_C = OFF_BG + 4 * ML_HEADS
OFF_G = OFF_C + 2 * SG_WIDTH
D_IN = OFF_G + N_BRANCH * D_MODEL
N_GATE = 4 * ML_HEADS

P_A = 0
P_B = P_A + 3 * NA_WIDTH
P_C = P_B + 4 * ML_WIDTH
P_G = P_C + 2 * SG_WIDTH
P_WIDTH = P_G + N_BRANCH * D_MODEL

LANE = 128
HALO_ROWS = 16
VMEM_LIMIT = 56 * 1024 * 1024


def _cparams(sem, vmem=VMEM_LIMIT, **kw):
    return pltpu.CompilerParams(dimension_semantics=sem, vmem_limit_bytes=vmem, **kw)


def _rms(x, g):
    return x * lax.rsqrt(jnp.mean(x * x, axis=-1, keepdims=True) + EPS) * g


def _sigmoid(x):
    return 1.0 / (1.0 + jnp.exp(-x))


def _log_sigmoid(x):
    return jnp.minimum(x, 0.0) - jnp.log(1.0 + jnp.exp(-jnp.abs(x)))


def _gelu_tanh(x):
    c = np.float32(np.sqrt(2.0 / np.pi))
    return 0.5 * x * (1.0 + jnp.tanh(c * (x + 0.044715 * (x * x * x))))


def _dot(a, b):
    return jnp.dot(a, b, preferred_element_type=F32)


def _dot_nt(a, b):
    return lax.dot_general(a, b, (((1,), (1,)), ((), ())), preferred_element_type=F32)


def _dot_tn(a, b):
    return lax.dot_general(a, b, (((0,), (0,)), ((), ())), preferred_element_type=F32)


def _inproj_kernel(x_ref, g_ref, w_ref, wg_ref, wgt_ref, bgc_ref, bgr_ref,
                   o_ref, gc_ref, gr_ref, h_scr):
    @pl.when(pl.program_id(1) == 0)
    def _():
        hb = _rms(x_ref[...], g_ref[...]).astype(BF16)
        h_scr[...] = hb
        gc_ref[...] = _dot(hb, wg_ref[...]) + bgc_ref[...]
        gr_ref[...] = _dot_nt(wgt_ref[...], hb) + bgr_ref[...]

    o_ref[...] = _dot(h_scr[...], w_ref[...]).astype(BF16)


def _in_proj(xf, g, w_main, w_gate, w_gate_t, bg_col, bg_row, tm=1024, tn=1024):
    n = xf.shape[0]
    tm = min(tm, n)
    grid = (n // tm, P_WIDTH // tn)
    return pl.pallas_call(
        _inproj_kernel,
        grid=grid,
        in_specs=[
            pl.BlockSpec((tm, D_MODEL), lambda i, j: (i, 0)),
            pl.BlockSpec((1, D_MODEL), lambda i, j: (0, 0)),
            pl.BlockSpec((D_MODEL, tn), lambda i, j: (0, j)),
            pl.BlockSpec((D_MODEL, LANE), lambda i, j: (0, 0)),
            pl.BlockSpec((N_GATE, D_MODEL), lambda i, j: (0, 0)),
            pl.BlockSpec((1, LANE), lambda i, j: (0, 0)),
            pl.BlockSpec((N_GATE, 1), lambda i, j: (0, 0)),
        ],
        out_specs=[
            pl.BlockSpec((tm, tn), lambda i, j: (i, j)),
            pl.BlockSpec((tm, LANE), lambda i, j: (i, 0)),
            pl.BlockSpec((N_GATE, tm), lambda i, j: (0, i)),
        ],
        out_shape=[
            jax.ShapeDtypeStruct((n, P_WIDTH), BF16),
            jax.ShapeDtypeStruct((n, LANE), F32),
            jax.ShapeDtypeStruct((N_GATE, n), F32),
        ],
        scratch_shapes=[pltpu.VMEM((tm, D_MODEL), BF16)],
        compiler_params=_cparams(("parallel", "arbitrary")),
        name="in_proj",
    )(xf, g, w_main, w_gate, w_gate_t, bg_col, bg_row)


def _na_bias_table(rpb):
    qc = np.arange(GRID_W)[:, None]
    kc = np.arange(GRID_W)[None, :]
    cs = np.clip(qc - NA_KC // 2, 0, GRID_W - NA_KC)
    ok = (kc >= cs) & (kc < cs + NA_KC)
    dc = np.clip(kc - qc + NA_KC - 1, 0, 2 * NA_KC - 2)
    d = np.arange(NA_KR)[:, None] + np.arange(NA_KR)[None, :]
    tab = rpb.astype(F32)[:, d][:, :, :, dc]
    tab = jnp.where(jnp.asarray(ok)[None, None, None], tab, -1e30)
    tab = tab.transpose(0, 1, 3, 2, 4)
    return tab.reshape(NA_HEADS, NA_KR, GRID_W, NA_KR * GRID_W)


def _na_kernel(q_ref, k_ref, v_ref, bias_ref, o_ref, *, rows):
    scale = np.float32(NA_HEAD_DIM ** -0.5)
    win = NA_KR * GRID_W

    def body(r, carry):
        rs = jnp.clip(r - NA_KR // 2, 0, rows - NA_KR)
        q0 = pl.multiple_of(r * GRID_W, GRID_W)
        k0 = pl.multiple_of(rs * GRID_W, GRID_W)
        q = q_ref[pl.ds(q0, GRID_W), :]
        kw = k_ref[pl.ds(k0, win), :]
        vw = v_ref[pl.ds(k0, win), :]
        s = _dot_nt(q, kw) * scale + bias_ref[rs - r + NA_KR - 1]
        m = jnp.max(s, axis=-1, keepdims=True)
        p = jnp.exp(s - m)
        l = jnp.sum(p, axis=-1, keepdims=True)
        o = _dot(p.astype(BF16), vw) * (1.0 / l)
        o_ref[pl.ds(q0, GRID_W), :] = o.astype(BF16)
        return carry

    lax.fori_loop(0, rows, body, 0)


def _na(proj3, bias_tab):
    b, t, _ = proj3.shape
    rows = t // GRID_W
    blk = lambda off: pl.BlockSpec((None, t, NA_HEAD_DIM), lambda bi, h: (bi, 0, off + h))
    return pl.pallas_call(
        functools.partial(_na_kernel, rows=rows),
        grid=(b, NA_HEADS),
        in_specs=[
            blk(P_A // NA_HEAD_DIM),
            blk(P_A // NA_HEAD_DIM + NA_HEADS),
            blk(P_A // NA_HEAD_DIM + 2 * NA_HEADS),
            pl.BlockSpec((None, NA_KR, GRID_W, NA_KR * GRID_W), lambda bi, h: (h, 0, 0, 0)),
        ],
        out_specs=pl.BlockSpec((None, t, NA_HEAD_DIM), lambda bi, h: (bi, 0, h)),
        out_shape=jax.ShapeDtypeStruct((b, t, NA_WIDTH), BF16),
        compiler_params=_cparams(("parallel", "parallel")),
        name="na",
    )(proj3, proj3, proj3, bias_tab)


def _mlstm_kernel(qp_ref, q_ref, qn_ref, kp_ref, k_ref, kn_ref, v_ref, gc_ref, gr_ref, cw_ref,
                  h_ref, c_scr, n_scr, m_scr, *, reverse, nc, L):
    c = pl.program_id(1)
    ce = (nc - 1 - c) if reverse else c

    @pl.when(c == 0)
    def _():
        c_scr[...] = jnp.zeros_like(c_scr)
        n_scr[...] = jnp.zeros_like(n_scr)
        m_scr[...] = jnp.zeros_like(m_scr)

    rowi = lax.broadcasted_iota(jnp.int32, (L, 1), 0)
    has_prev = (ce > 0).astype(F32)
    has_next = (ce < nc - 1).astype(F32)

    def conv_silu(p_ref, x_ref, n_ref, w):
        x = x_ref[...].astype(F32)
        prev_row = p_ref[HALO_ROWS - 1:HALO_ROWS, :].astype(F32) * has_prev
        next_row = n_ref[0:1, :].astype(F32) * has_next
        xp = jnp.where(rowi == 0, prev_row, pltpu.roll(x, 1, 0))
        xn = jnp.where(rowi == L - 1, next_row, pltpu.roll(x, L - 1, 0))
        y = xp * w[0:1] + x * w[1:2] + xn * w[2:3]
        return y * _sigmoid(y)

    cw = cw_ref[...]
    qs = conv_silu(qp_ref, q_ref, qn_ref, cw[:, :ML_WIDTH]) * np.float32(ML_HEAD_DIM ** -0.5)
    ks = conv_silu(kp_ref, k_ref, kn_ref, cw[:, ML_WIDTH:])

    ri = lax.broadcasted_iota(jnp.int32, (L, L), 0)
    ci = lax.broadcasted_iota(jnp.int32, (L, L), 1)
    if reverse:
        mask = ci >= ri
        mask_t = ri >= ci
    else:
        mask = ci <= ri
        mask_t = ri <= ci
    gi = 2 if reverse else 0

    for hd in range(ML_HEADS):
        sl = slice(hd * ML_HEAD_DIM, (hd + 1) * ML_HEAD_DIM)
        ii = gi * ML_HEADS + hd
        fi = (gi + 1) * ML_HEADS + hd
        i_row = gr_ref[ii:ii + 1, :]
        i_col = gc_ref[:, ii:ii + 1]
        lf_row = _log_sigmoid(gr_ref[fi:fi + 1, :])
        lf_col = _log_sigmoid(gc_ref[:, fi:fi + 1])
        b_col = jnp.sum(jnp.where(mask, lf_row, 0.0), axis=1, keepdims=True)
        b_row = jnp.sum(jnp.where(mask_t, lf_col, 0.0), axis=0, keepdims=True)
        b_last = jnp.sum(lf_row, axis=1, keepdims=True)

        m_prev = m_scr[hd][:, 0:1]
        c_prev = c_scr[hd]
        n_prev = n_scr[hd]

        qh = qs[:, sl]
        kh = ks[:, sl]
        qb = qh.astype(BF16)
        vb = v_ref[:, sl]

        dlog = jnp.where(mask, b_col - b_row + i_row, -1e30)
        inter = b_col + m_prev
        m_row = jnp.maximum(inter, jnp.max(dlog, axis=1, keepdims=True))
        s = _dot_nt(qb, kh.astype(BF16)) * jnp.exp(dlog - m_row)
        w_inter = jnp.exp(inter - m_row)
        num = _dot(s.astype(BF16), vb) + w_inter * _dot(qb, c_prev.astype(BF16))
        nq = jnp.sum(s, axis=1, keepdims=True) + w_inter * jnp.sum(qh * n_prev, axis=1, keepdims=True)
        h = num * (1.0 / jnp.maximum(jnp.abs(nq), jnp.exp(-m_row)))
        h_ref[:, sl] = h

        wlog = b_last - b_col + i_col
        m_new = jnp.maximum(b_last + m_prev, jnp.max(wlog, axis=0, keepdims=True))
        w_col = jnp.exp(wlog - m_new)
        decay = jnp.exp(b_last + m_prev - m_new)
        kw = kh * w_col
        c_scr[hd] = decay * c_prev + _dot_tn(kw.astype(BF16), vb)
        n_scr[hd] = decay * n_prev + jnp.sum(kw, axis=0, keepdims=True)
        m_scr[hd] = jnp.broadcast_to(m_new, (1, LANE))


def _mlstm(proj3, gate_col, gate_row, conv_w, *, reverse):
    b, t, _ = proj3.shape
    L = ML_CHUNK
    nc = t // L
    hb = L // HALO_ROWS
    n_halo = t // HALO_ROWS

    def ce(c):
        return (nc - 1 - c) if reverse else c

    cur = lambda col: pl.BlockSpec((None, L, ML_WIDTH), lambda bi, c: (bi, ce(c), col))
    prv = lambda col: pl.BlockSpec((None, HALO_ROWS, ML_WIDTH),
                                   lambda bi, c: (bi, jnp.maximum(ce(c) * hb - 1, 0), col))
    nxt = lambda col: pl.BlockSpec((None, HALO_ROWS, ML_WIDTH),
                                   lambda bi, c: (bi, jnp.minimum((ce(c) + 1) * hb, n_halo - 1), col))
    qc, kc, vc = P_B // ML_WIDTH, P_B // ML_WIDTH + 1, P_B // ML_WIDTH + 2
    return pl.pallas_call(
        functools.partial(_mlstm_kernel, reverse=reverse, nc=nc, L=L),
        grid=(b, nc),
        in_specs=[
            prv(qc), cur(qc), nxt(qc),
            prv(kc), cur(kc), nxt(kc),
            cur(vc),
            pl.BlockSpec((L, LANE), lambda bi, c: (bi * nc + ce(c), 0)),
            pl.BlockSpec((N_GATE, L), lambda bi, c: (0, bi * nc + ce(c))),
            pl.BlockSpec((3, 2 * ML_WIDTH), lambda bi, c: (0, 0)),
        ],
        out_specs=pl.BlockSpec((None, L, ML_WIDTH), lambda bi, c: (bi, ce(c), 0)),
        out_shape=jax.ShapeDtypeStruct((b, t, ML_WIDTH), F32),
        scratch_shapes=[
            pltpu.VMEM((ML_HEADS, ML_HEAD_DIM, ML_HEAD_DIM), F32),
            pltpu.VMEM((ML_HEADS, 1, ML_HEAD_DIM), F32),
            pltpu.VMEM((ML_HEADS, 1, LANE), F32),
        ],
        compiler_params=_cparams(("parallel", "arbitrary")),
        name="mlstm_bwd" if reverse else "mlstm_fwd",
    )(proj3, proj3, proj3, proj3, proj3, proj3, proj3, gate_col, gate_row, conv_w)


def _ml_merge_kernel(hf_ref, hb_ref, o_ref, g_ref, y_ref):
    for hd in range(ML_HEADS):
        sl = slice(hd * ML_HEAD_DIM, (hd + 1) * ML_HEAD_DIM)
        h = hf_ref[:, sl] + hb_ref[:, sl]
        y = _rms(h, g_ref[:, sl]) * _sigmoid(o_ref[:, sl].astype(F32))
        y_ref[:, sl] = y.astype(BF16)


def _ml_merge(hf, hb, proj, g, tm=512):
    n = proj.shape[0]
    row = lambda col: pl.BlockSpec((tm, ML_WIDTH), lambda i: (i, col))
    return pl.pallas_call(
        _ml_merge_kernel,
        grid=(n // tm,),
        in_specs=[row(0), row(0), row(P_B // ML_WIDTH + 3),
                  pl.BlockSpec((1, ML_WIDTH), lambda i: (0, 0))],
        out_specs=row(0),
        out_shape=jax.ShapeDtypeStruct((n, ML_WIDTH), BF16),
        compiler_params=_cparams(("parallel",)),
        name="ml_merge",
    )(hf, hb, proj, g)


def _sg_kernel(u_ref, v_ref, g_ref, ws_ref, bst_ref, y_ref, *, chunks):
    for cidx in range(chunks):
        rs = slice(cidx * SG_CHUNK, (cidx + 1) * SG_CHUNK)
        zu = _gelu_tanh(u_ref[rs, :].astype(F32))
        zv = _gelu_tanh(v_ref[rs, :].astype(F32))
        vb = _rms(zv, g_ref[...]).astype(BF16)
        for gidx in range(SG_GROUPS):
            cs = slice(gidx * SG_GROUP_DIM, (gidx + 1) * SG_GROUP_DIM)
            mixed = _dot(ws_ref[gidx], vb[:, cs]) + bst_ref[:, gidx:gidx + 1]
            y_ref[rs, cs] = (zu[:, cs] * mixed).astype(BF16)


def _sg(proj, g, ws, bst, chunks=4):
    n = proj.shape[0]
    tm = chunks * SG_CHUNK
    row = lambda col: pl.BlockSpec((tm, SG_WIDTH), lambda i: (i, col))
    return pl.pallas_call(
        functools.partial(_sg_kernel, chunks=chunks),
        grid=(n // tm,),
        in_specs=[row(P_C // SG_WIDTH), row(P_C // SG_WIDTH + 1),
                  pl.BlockSpec((1, SG_WIDTH), lambda i: (0, 0)),
                  pl.BlockSpec((SG_GROUPS, SG_CHUNK, SG_CHUNK), lambda i: (0, 0, 0)),
                  pl.BlockSpec((SG_CHUNK, SG_GROUPS), lambda i: (0, 0))],
        out_specs=row(0),
        out_shape=jax.ShapeDtypeStruct((n, SG_WIDTH), BF16),
        compiler_params=_cparams(("parallel",)),
        name="sg",
    )(proj, proj, g, ws, bst)


def _mix_kernel(ya_ref, yb_ref, yc_ref, ga_ref, gb_ref, gc_ref, wa_ref, wb_ref, wc_ref,
                wo_ref, x_ref, g_ref, o_ref, m_scr, *, nj, tn):
    j = pl.program_id(1)

    def branch(y_ref, w_ref, gate_ref):
        return _sigmoid(gate_ref[...].astype(F32)) * _dot(y_ref[...], w_ref[...])

    merged = branch(ya_ref, wa_ref, ga_ref) + branch(yb_ref, wb_ref, gb_ref) + branch(yc_ref, wc_ref, gc_ref)
    m_scr[j] = merged.astype(BF16)

    @pl.when(j == nj - 1)
    def _():
        acc = _dot(m_scr[0], wo_ref[0:tn, :])
        for jj in range(1, nj):
            acc = acc + _dot(m_scr[jj], wo_ref[jj * tn:(jj + 1) * tn, :])
        o_ref[...] = x_ref[...] + _rms(acc, g_ref[...])


def _mix_out(ya, yb, yc, proj, wa, wb, wc, wo, xf, g, tm=512, tn=512):
    n = xf.shape[0]
    nj = D_MODEL // tn
    yspec = pl.BlockSpec((tm, NA_WIDTH), lambda i, j: (i, 0))
    gate = lambda k: pl.BlockSpec((tm, tn), lambda i, j: (i, (P_G + k * D_MODEL) // tn + j))
    wspec = pl.BlockSpec((NA_WIDTH, tn), lambda i, j: (0, j))
    return pl.pallas_call(
        functools.partial(_mix_kernel, nj=nj, tn=tn),
        grid=(n // tm, nj),
        in_specs=[yspec, yspec, yspec, gate(0), gate(1), gate(2), wspec, wspec, wspec,
                  pl.BlockSpec((D_MODEL, D_MODEL), lambda i, j: (0, 0), pipeline_mode=pl.Buffered(1)),
                  pl.BlockSpec((tm, D_MODEL), lambda i, j: (i, 0)),
                  pl.BlockSpec((1, D_MODEL), lambda i, j: (0, 0))],
        out_specs=pl.BlockSpec((tm, D_MODEL), lambda i, j: (i, 0)),
        out_shape=jax.ShapeDtypeStruct((n, D_MODEL), F32),
        scratch_shapes=[pltpu.VMEM((nj, tm, tn), BF16)],
        compiler_params=_cparams(("parallel", "arbitrary")),
        name="mix_out",
    )(ya, yb, yc, proj, proj, proj, wa, wb, wc, wo, xf, g)


def _mlp_kernel(x_ref, g1_ref, wu_ref, wd_ref, g2_ref, o_ref, h_scr, acc_scr, *, nj):
    j = pl.program_id(1)

    @pl.when(j == 0)
    def _():
        h_scr[...] = _rms(x_ref[...], g1_ref[...]).astype(BF16)
        acc_scr[...] = jnp.zeros_like(acc_scr)

    a = jnp.maximum(_dot(h_scr[...], wu_ref[...]), 0.0)
    acc_scr[...] += _dot((a * a).astype(BF16), wd_ref[...])

    @pl.when(j == nj - 1)
    def _():
        o_ref[...] = x_ref[...] + _rms(acc_scr[...], g2_ref[...])


def _mlp(xf, g1, wu, wd, g2, tm=512, tf=1024):
    n = xf.shape[0]
    nj = D_FF // tf
    return pl.pallas_call(
        functools.partial(_mlp_kernel, nj=nj),
        grid=(n // tm, nj),
        in_specs=[pl.BlockSpec((tm, D_MODEL), lambda i, j: (i, 0)),
                  pl.BlockSpec((1, D_MODEL), lambda i, j: (0, 0)),
                  pl.BlockSpec((D_MODEL, tf), lambda i, j: (0, j)),
                  pl.BlockSpec((tf, D_MODEL), lambda i, j: (j, 0)),
                  pl.BlockSpec((1, D_MODEL), lambda i, j: (0, 0))],
        out_specs=pl.BlockSpec((tm, D_MODEL), lambda i, j: (i, 0)),
        out_shape=jax.ShapeDtypeStruct((n, D_MODEL), F32),
        scratch_shapes=[pltpu.VMEM((tm, D_MODEL), BF16), pltpu.VMEM((tm, D_MODEL), F32)],
        compiler_params=_cparams(("parallel", "arbitrary")),
        name="mlp",
    )(xf, g1, wu, wd, g2)


def _prep_layer(pre_mix_g, post_mix_g, pre_mlp_g, post_mlp_g, w_in, b_gate, conv_w, na_rpb,
                ml_norm_g, sg_norm_g, w_s, b_s, w_a, w_b, w_c, w_out, w_up, w_down):
    w_main = jnp.concatenate([w_in[:, :OFF_BG], w_in[:, OFF_C:]], axis=1).astype(BF16)
    w_g = w_in[:, OFF_BG:OFF_C].astype(BF16)
    w_gate = jnp.pad(w_g, ((0, 0), (0, LANE - N_GATE)))
    bg_col = jnp.pad(b_gate.astype(F32), (0, LANE - N_GATE)).reshape(1, LANE)
    return dict(
        pre_mix_g=pre_mix_g.reshape(1, D_MODEL), post_mix_g=post_mix_g.reshape(1, D_MODEL),
        pre_mlp_g=pre_mlp_g.reshape(1, D_MODEL), post_mlp_g=post_mlp_g.reshape(1, D_MODEL),
        w_main=w_main, w_gate=w_gate, w_gate_t=w_g.T, bg_col=bg_col,
        bg_row=b_gate.astype(F32).reshape(N_GATE, 1),
        conv_w=conv_w.astype(F32), bias_tab=_na_bias_table(na_rpb),
        ml_norm_g=ml_norm_g.reshape(1, ML_WIDTH), sg_norm_g=sg_norm_g.reshape(1, SG_WIDTH),
        w_s=w_s.astype(BF16), bst=b_s.astype(F32).T,
        w_a=w_a.astype(BF16), w_b=w_b.astype(BF16), w_c=w_c.astype(BF16),
        w_out=w_out.astype(BF16), w_up=w_up.astype(BF16), w_down=w_down.astype(BF16),
    )


def _trunk_layer(x, p):
    b, t, _ = x.shape
    n = b * t
    xf = x.reshape(n, D_MODEL)
    proj, gate_col, gate_row = _in_proj(xf, p["pre_mix_g"], p["w_main"], p["w_gate"], p["w_gate_t"],
                                        p["bg_col"], p["bg_row"])
    proj3 = proj.reshape(b, t, P_WIDTH)
    y_a = _na(proj3, p["bias_tab"]).reshape(n, NA_WIDTH)
    h_f = _mlstm(proj3, gate_col, gate_row, p["conv_w"], reverse=False).reshape(n, ML_WIDTH)
    h_b = _mlstm(proj3, gate_col, gate_row, p["conv_w"], reverse=True).reshape(n, ML_WIDTH)
    y_b = _ml_merge(h_f, h_b, proj, p["ml_norm_g"])
    y_c = _sg(proj, p["sg_norm_g"], p["w_s"], p["bst"])
    x1 = _mix_out(y_a, y_b, y_c, proj, p["w_a"], p["w_b"], p["w_c"], p["w_out"], xf, p["post_mix_g"])
    x2 = _mlp(x1, p["pre_mlp_g"], p["w_up"], p["w_down"], p["post_mlp_g"])
    return x2.reshape(b, t, D_MODEL)


def kernel(x_prompt, x_sample, pre_mix_g, post_mix_g, pre_mlp_g, post_mlp_g, w_in, b_gate, conv_w,
           na_rpb, ml_norm_g, sg_norm_g, w_s, b_s, w_a, w_b, w_c, w_out, w_up, w_down):
    depth = w_in.shape[0]
    layers = [
        _prep_layer(pre_mix_g[l], post_mix_g[l], pre_mlp_g[l], post_mlp_g[l], w_in[l], b_gate[l],
                    conv_w[l], na_rpb[l], ml_norm_g[l], sg_norm_g[l], w_s[l], b_s[l], w_a[l], w_b[l],
                    w_c[l], w_out[l], w_up[l], w_down[l])
        for l in range(depth)
    ]

    def run(x):
        for p in layers:
            x = _trunk_layer(x, p)
        return x

    return (run(x_prompt), run(x_sample))
```

```python
import functools

import numpy as np
import jax
import jax.numpy as jnp
from jax import lax
from jax.experimental import pallas as pl
from jax.experimental.pallas import tpu as pltpu

F32 = jnp.float32
BF16 = jnp.bfloat16

D_MODEL = 2048
GRID_W = 64
EPS = 1e-6

NA_HEADS = 8
NA_HEAD_DIM = 128
NA_WIDTH = NA_HEADS * NA_HEAD_DIM
NA_KR = 8
NA_KC = 16
NA_ROWS_PER_STEP = 8

ML_HEADS = 4
ML_HEAD_DIM = 256
ML_WIDTH = ML_HEADS * ML_HEAD_DIM
ML_CHUNK = 256

SG_GROUPS = 8
SG_CHUNK = 128
SG_WIDTH = 1024
SG_GROUP_DIM = SG_WIDTH // SG_GROUPS

D_FF = 4 * D_MODEL
N_BRANCH = 3

OFF_A = 0
OFF_B = OFF_A + 3 * NA_WIDTH
OFF_BG = OFF_B + 4 * ML_WIDTH
OFF_C = OFF_BG + 4 * ML_HEADS
OFF_G = OFF_C + 2 * SG_WIDTH
D_IN = OFF_G + N_BRANCH * D_MODEL
N_GATE = 4 * ML_HEADS

P_A = 0
P_B = P_A + 3 * NA_WIDTH
P_C = P_B + 4 * ML_WIDTH
P_G = P_C + 2 * SG_WIDTH
P_WIDTH = P_G + N_BRANCH * D_MODEL

LANE = 128
HALO_ROWS = 16
VMEM_LIMIT = 56 * 1024 * 1024
LOG2E = np.float32(1.4426950408889634)


def _cparams(sem, vmem=VMEM_LIMIT, **kw):
    return pltpu.CompilerParams(dimension_semantics=sem, vmem_limit_bytes=vmem, **kw)


def _rms(x, g):
    return x * lax.rsqrt(jnp.mean(x * x, axis=-1, keepdims=True) + EPS) * g


def _sigmoid(x):
    return 1.0 / (1.0 + jnp.exp(-x))


def _log_sigmoid(x):
    return jnp.minimum(x, 0.0) - jnp.log(1.0 + jnp.exp(-jnp.abs(x)))


def _gelu_tanh(x):
    c = np.float32(np.sqrt(2.0 / np.pi))
    return 0.5 * x * (1.0 + jnp.tanh(c * (x + 0.044715 * (x * x * x))))


def _dot(a, b):
    return jnp.dot(a, b, preferred_element_type=F32)


def _dot_nt(a, b):
    return lax.dot_general(a, b, (((1,), (1,)), ((), ())), preferred_element_type=F32)


def _dot_tn(a, b):
    return lax.dot_general(a, b, (((0,), (0,)), ((), ())), preferred_element_type=F32)


def _inproj_kernel(x_ref, g_ref, w_ref, wg_ref, wgt_ref, bgc_ref, bgr_ref,
                   o_ref, gc_ref, gr_ref, h_scr):
    @pl.when(pl.program_id(1) == 0)
    def _():
        hb = _rms(x_ref[...], g_ref[...]).astype(BF16)
        h_scr[...] = hb
        gc_ref[...] = _dot(hb, wg_ref[...]) + bgc_ref[...]
        gr_ref[...] = _dot_nt(wgt_ref[...], hb) + bgr_ref[...]

    o_ref[...] = _dot(h_scr[...], w_ref[...]).astype(BF16)


def _in_proj(xf, g, w_main, w_gate, w_gate_t, bg_col, bg_row, tm=1024, tn=1024):
    n = xf.shape[0]
    tm = min(tm, n)
    grid = (n // tm, P_WIDTH // tn)
    return pl.pallas_call(
        _inproj_kernel,
        grid=grid,
        in_specs=[
            pl.BlockSpec((tm, D_MODEL), lambda i, j: (i, 0)),
            pl.BlockSpec((1, D_MODEL), lambda i, j: (0, 0)),
            pl.BlockSpec((D_MODEL, tn), lambda i, j: (0, j)),
            pl.BlockSpec((D_MODEL, LANE), lambda i, j: (0, 0)),
            pl.BlockSpec((N_GATE, D_MODEL), lambda i, j: (0, 0)),
            pl.BlockSpec((1, LANE), lambda i, j: (0, 0)),
            pl.BlockSpec((N_GATE, 1), lambda i, j: (0, 0)),
        ],
        out_specs=[
            pl.BlockSpec((tm, tn), lambda i, j: (i, j)),
            pl.BlockSpec((tm, LANE), lambda i, j: (i, 0)),
            pl.BlockSpec((N_GATE, tm), lambda i, j: (0, i)),
        ],
        out_shape=[
            jax.ShapeDtypeStruct((n, P_WIDTH), BF16),
            jax.ShapeDtypeStruct((n, LANE), F32),
            jax.ShapeDtypeStruct((N_GATE, n), F32),
        ],
        scratch_shapes=[pltpu.VMEM((tm, D_MODEL), BF16)],
        compiler_params=_cparams(("parallel", "arbitrary")),
        name="in_proj",
    )(xf, g, w_main, w_gate, w_gate_t, bg_col, bg_row)


def _na_bias_table(rpb):
    qc = np.arange(GRID_W)[:, None]
    kc = np.arange(GRID_W)[None, :]
    cs = np.clip(qc - NA_KC // 2, 0, GRID_W - NA_KC)
    ok = (kc >= cs) & (kc < cs + NA_KC)
    dc = np.clip(kc - qc + NA_KC - 1, 0, 2 * NA_KC - 2)
    d = np.arange(NA_KR)[:, None] + np.arange(NA_KR)[None, :]
    tab = rpb.astype(F32)[:, d][:, :, :, dc]
    tab = jnp.where(jnp.asarray(ok)[None, None, None], tab, -1e30)
    tab = tab.transpose(0, 1, 3, 2, 4)
    return tab.reshape(NA_HEADS, NA_KR, GRID_W, NA_KR * GRID_W)


def _na_kernel(q_ref, k_ref, v_ref, bias_ref, o_ref, *, rows):
    scale = np.float32(NA_HEAD_DIM ** -0.5)
    win = NA_KR * GRID_W

    def body(i, carry):
        rr = [i * NA_ROWS_PER_STEP + u for u in range(NA_ROWS_PER_STEP)]
        rs = [jnp.clip(r - NA_KR // 2, 0, rows - NA_KR) for r in rr]
        q0 = [pl.multiple_of(r * GRID_W, GRID_W) for r in rr]
        k0 = [pl.multiple_of(x * GRID_W, GRID_W) for x in rs]
        s = [_dot_nt(q_ref[pl.ds(q0[u], GRID_W), :], k_ref[pl.ds(k0[u], win), :])
             for u in range(NA_ROWS_PER_STEP)]
        p, inv = [], []
        for u in range(NA_ROWS_PER_STEP):
            su = s[u] * scale + bias_ref[rs[u] - rr[u] + NA_KR - 1]
            e = jnp.exp(su - jnp.max(su, axis=-1, keepdims=True))
            inv.append(1.0 / jnp.sum(e, axis=-1, keepdims=True))
            p.append(e.astype(BF16))
        for u in range(NA_ROWS_PER_STEP):
            o = _dot(p[u], v_ref[pl.ds(k0[u], win), :]) * inv[u]
            o_ref[pl.ds(q0[u], GRID_W), :] = o.astype(BF16)
        return carry

    lax.fori_loop(0, rows // NA_ROWS_PER_STEP, body, 0)


def _na(proj3, bias_tab):
    b, t, _ = proj3.shape
    rows = t // GRID_W
    blk = lambda off: pl.BlockSpec((None, t, NA_HEAD_DIM), lambda bi, h: (bi, 0, off + h))
    return pl.pallas_call(
        functools.partial(_na_kernel, rows=rows),
        grid=(b, NA_HEADS),
        in_specs=[
            blk(P_A // NA_HEAD_DIM),
            blk(P_A // NA_HEAD_DIM + NA_HEADS),
            blk(P_A // NA_HEAD_DIM + 2 * NA_HEADS),
            pl.BlockSpec((None, NA_KR, GRID_W, NA_KR * GRID_W), lambda bi, h: (h, 0, 0, 0)),
        ],
        out_specs=pl.BlockSpec((None, t, NA_HEAD_DIM), lambda bi, h: (bi, 0, h)),
        out_shape=jax.ShapeDtypeStruct((b, t, NA_WIDTH), BF16),
        compiler_params=_cparams(("parallel", "parallel")),
        name="na",
    )(proj3, proj3, proj3, bias_tab)


def _tri_product(tri_b, x, *, left):
    hi = x.astype(BF16)
    rest = x - hi.astype(F32)
    mid = rest.astype(BF16)
    lo = (rest - mid.astype(F32)).astype(BF16)
    if left:
        return (_dot(tri_b, lo) + _dot(tri_b, mid)) + _dot(tri_b, hi)
    return (_dot(lo, tri_b) + _dot(mid, tri_b)) + _dot(hi, tri_b)


def _mlstm_init(c_scr, n_scr, m_scr):
    @pl.when(pl.program_id(1) == 0)
    def _():
        c_scr[...] = jnp.zeros_like(c_scr)
        n_scr[...] = jnp.zeros_like(n_scr)
        m_scr[...] = jnp.zeros_like(m_scr)


def _mlstm_chunk(q_of, k_of, v_ref, gc_ref, gr_ref, c_scr, n_scr, m_scr, emit, *, reverse, L):
    ri = lax.broadcasted_iota(jnp.int32, (L, L), 0)
    ci = lax.broadcasted_iota(jnp.int32, (L, L), 1)
    mask = (ci >= ri) if reverse else (ci <= ri)
    mask_t = (ri >= ci) if reverse else (ri <= ci)
    gi = 2 if reverse else 0
    last = 0 if reverse else L - 1

    gcol = gc_ref[...]
    grow = gr_ref[...]
    b_col_all = _tri_product(jnp.where(mask, 1.0, 0.0).astype(BF16), _log_sigmoid(gcol), left=True)
    b_row_all = _tri_product(jnp.where(mask_t, 1.0, 0.0).astype(BF16), _log_sigmoid(grow), left=False)

    for hd in range(ML_HEADS):
        sl = slice(hd * ML_HEAD_DIM, (hd + 1) * ML_HEAD_DIM)
        ii = gi * ML_HEADS + hd
        fi = (gi + 1) * ML_HEADS + hd
        i_row = grow[ii:ii + 1, :]
        i_col = gcol[:, ii:ii + 1]
        b_row = b_row_all[fi:fi + 1, :]
        b_col = b_col_all[:, fi:fi + 1]
        b_last = b_col_all[last:last + 1, fi:fi + 1]

        m_prev = m_scr[hd][:, 0:1]
        c_prev = c_scr[hd]
        n_prev = n_scr[hd]
        qf, qb = q_of(hd)
        kf, kb = k_of(hd)
        vb = v_ref[:, sl]

        d2 = jnp.where(mask, b_col * LOG2E + (i_row - b_row) * LOG2E, -1e30)
        inter = b_col + m_prev
        m2 = jnp.maximum(inter * LOG2E, jnp.max(d2, axis=1, keepdims=True))
        s = _dot_nt(qb, kb) * jnp.exp2(d2 - m2)
        w_inter = jnp.exp2(inter * LOG2E - m2)
        num = _dot(s.astype(BF16), vb) + w_inter * _dot(qb, c_prev.astype(BF16))
        nq = jnp.sum(s, axis=1, keepdims=True) + w_inter * jnp.sum(qf * n_prev, axis=1, keepdims=True)
        emit(hd, num * (1.0 / jnp.maximum(jnp.abs(nq), jnp.exp2(-m2))))

        wlog = b_last - b_col + i_col
        m_new = jnp.maximum(b_last + m_prev, jnp.max(wlog, axis=0, keepdims=True))
        w_col = jnp.exp(wlog - m_new)
        decay = jnp.exp(b_last + m_prev - m_new)
        kw = kf * w_col
        c_scr[hd] = decay * c_prev + _dot_tn(kw.astype(BF16), vb)
        n_scr[hd] = decay * n_prev + jnp.sum(kw, axis=0, keepdims=True)
        m_scr[hd] = jnp.broadcast_to(m_new, (1, LANE))


def _mlstm_fwd_kernel(qp_ref, q_ref, qn_ref, kp_ref, k_ref, kn_ref, v_ref, gc_ref, gr_ref, cw_ref,
                      h_ref, qs_ref, ks_ref, c_scr, n_scr, m_scr, *, nc, L):
    c = pl.program_id(1)
    _mlstm_init(c_scr, n_scr, m_scr)

    rowi = lax.broadcasted_iota(jnp.int32, (L, 1), 0)
    has_prev = (c > 0).astype(F32)
    has_next = (c < nc - 1).astype(F32)

    def conv_silu(p_ref, x_ref, n_ref, w):
        x = x_ref[...].astype(F32)
        prev_row = p_ref[HALO_ROWS - 1:HALO_ROWS, :].astype(F32) * has_prev
        next_row = n_ref[0:1, :].astype(F32) * has_next
        xp = jnp.where(rowi == 0, prev_row, pltpu.roll(x, 1, 0))
        xn = jnp.where(rowi == L - 1, next_row, pltpu.roll(x, L - 1, 0))
        y = xp * w[0:1] + x * w[1:2] + xn * w[2:3]
        return y * _sigmoid(y)

    cw = cw_ref[...]
    qs = conv_silu(qp_ref, q_ref, qn_ref, cw[:, :ML_WIDTH]) * np.float32(ML_HEAD_DIM ** -0.5)
    ks = conv_silu(kp_ref, k_ref, kn_ref, cw[:, ML_WIDTH:])
    qsb = qs.astype(BF16)
    ksb = ks.astype(BF16)
    qs_ref[...] = qsb
    ks_ref[...] = ksb

    def head(x, xb):
        return lambda hd: (x[:, hd * ML_HEAD_DIM:(hd + 1) * ML_HEAD_DIM],
                           xb[:, hd * ML_HEAD_DIM:(hd + 1) * ML_HEAD_DIM])

    def emit(hd, h):
        h_ref[:, hd * ML_HEAD_DIM:(hd + 1) * ML_HEAD_DIM] = h

    _mlstm_chunk(head(qs, qsb), head(ks, ksb), v_ref, gc_ref, gr_ref, c_scr, n_scr, m_scr, emit,
                 reverse=False, L=L)


def _mlstm_bwd_kernel(qs_ref, ks_ref, v_ref, gc_ref, gr_ref, hf_ref, o_ref, g_ref,
                      y_ref, c_scr, n_scr, m_scr, *, L):
    _mlstm_init(c_scr, n_scr, m_scr)

    def head(x_ref):
        def get(hd):
            xb = x_ref[:, hd * ML_HEAD_DIM:(hd + 1) * ML_HEAD_DIM]
            return xb.astype(F32), xb
        return get

    def emit(hd, h):
        sl = slice(hd * ML_HEAD_DIM, (hd + 1) * ML_HEAD_DIM)
        y = _rms(h + hf_ref[:, sl], g_ref[:, sl]) * _sigmoid(o_ref[:, sl].astype(F32))
        y_ref[:, sl] = y.astype(BF16)

    _mlstm_chunk(head(qs_ref), head(ks_ref), v_ref, gc_ref, gr_ref, c_scr, n_scr, m_scr, emit,
                 reverse=True, L=L)


def _mlstm_scratch():
    return [pltpu.VMEM((ML_HEADS, ML_HEAD_DIM, ML_HEAD_DIM), F32),
            pltpu.VMEM((ML_HEADS, 1, ML_HEAD_DIM), F32),
            pltpu.VMEM((ML_HEADS, 1, LANE), F32)]


def _mlstm_fwd(proj3, gate_col, gate_row, conv_w):
    b, t, _ = proj3.shape
    L = ML_CHUNK
    nc = t // L
    hb = L // HALO_ROWS
    n_halo = t // HALO_ROWS
    cur = lambda col: pl.BlockSpec((None, L, ML_WIDTH), lambda bi, c: (bi, c, col))
    prv = lambda col: pl.BlockSpec((None, HALO_ROWS, ML_WIDTH),
                                   lambda bi, c: (bi, jnp.maximum(c * hb - 1, 0), col))
    nxt = lambda col: pl.BlockSpec((None, HALO_ROWS, ML_WIDTH),
                                   lambda bi, c: (bi, jnp.minimum((c + 1) * hb, n_halo - 1), col))
    qc, kc, vc = P_B // ML_WIDTH, P_B // ML_WIDTH + 1, P_B // ML_WIDTH + 2
    return pl.pallas_call(
        functools.partial(_mlstm_fwd_kernel, nc=nc, L=L),
        grid=(b, nc),
        in_specs=[
            prv(qc), cur(qc), nxt(qc),
            prv(kc), cur(kc), nxt(kc),
            cur(vc),
            pl.BlockSpec((L, LANE), lambda bi, c: (bi * nc + c, 0)),
            pl.BlockSpec((N_GATE, L), lambda bi, c: (0, bi * nc + c)),
            pl.BlockSpec((3, 2 * ML_WIDTH), lambda bi, c: (0, 0)),
        ],
        out_specs=[cur(0), cur(0), cur(0)],
        out_shape=[jax.ShapeDtypeStruct((b, t, ML_WIDTH), F32),
                   jax.ShapeDtypeStruct((b, t, ML_WIDTH), BF16),
                   jax.ShapeDtypeStruct((b, t, ML_WIDTH), BF16)],
        scratch_shapes=_mlstm_scratch(),
        compiler_params=_cparams(("parallel", "arbitrary")),
        name="mlstm_fwd",
    )(proj3, proj3, proj3, proj3, proj3, proj3, proj3, gate_col, gate_row, conv_w)


def _mlstm_bwd(qs, ks, proj3, gate_col, gate_row, h_f, g):
    b, t, _ = proj3.shape
    L = ML_CHUNK
    nc = t // L
    cur = lambda col: pl.BlockSpec((None, L, ML_WIDTH), lambda bi, c: (bi, nc - 1 - c, col))
    vc, oc = P_B // ML_WIDTH + 2, P_B // ML_WIDTH + 3
    return pl.pallas_call(
        functools.partial(_mlstm_bwd_kernel, L=L),
        grid=(b, nc),
        in_specs=[
            cur(0), cur(0), cur(vc),
            pl.BlockSpec((L, LANE), lambda bi, c: (bi * nc + nc - 1 - c, 0)),
            pl.BlockSpec((N_GATE, L), lambda bi, c: (0, bi * nc + nc - 1 - c)),
            cur(0), cur(oc),
            pl.BlockSpec((1, ML_WIDTH), lambda bi, c: (0, 0)),
        ],
        out_specs=cur(0),
        out_shape=jax.ShapeDtypeStruct((b, t, ML_WIDTH), BF16),
        scratch_shapes=_mlstm_scratch(),
        compiler_params=_cparams(("parallel", "arbitrary")),
        name="mlstm_bwd",
    )(qs, ks, proj3, gate_col, gate_row, h_f, proj3, g)


def _sg_kernel(u_ref, v_ref, g_ref, ws_ref, bst_ref, y_ref, *, chunks):
    for cidx in range(chunks):
        rs = slice(cidx * SG_CHUNK, (cidx + 1) * SG_CHUNK)
        zu = _gelu_tanh(u_ref[rs, :].astype(F32))
        zv = _gelu_tanh(v_ref[rs, :].astype(F32))
        vb = _rms(zv, g_ref[...]).astype(BF16)
        for gidx in range(SG_GROUPS):
            cs = slice(gidx * SG_GROUP_DIM, (gidx + 1) * SG_GROUP_DIM)
            mixed = _dot(ws_ref[gidx], vb[:, cs]) + bst_ref[:, gidx:gidx + 1]
            y_ref[rs, cs] = (zu[:, cs] * mixed).astype(BF16)


def _sg(proj, g, ws, bst, chunks=4):
    n = proj.shape[0]
    tm = chunks * SG_CHUNK
    row = lambda col: pl.BlockSpec((tm, SG_WIDTH), lambda i: (i, col))
    return pl.pallas_call(
        functools.partial(_sg_kernel, chunks=chunks),
        grid=(n // tm,),
        in_specs=[row(P_C // SG_WIDTH), row(P_C // SG_WIDTH + 1),
                  pl.BlockSpec((1, SG_WIDTH), lambda i: (0, 0)),
                  pl.BlockSpec((SG_GROUPS, SG_CHUNK, SG_CHUNK), lambda i: (0, 0, 0)),
                  pl.BlockSpec((SG_CHUNK, SG_GROUPS), lambda i: (0, 0))],
        out_specs=row(0),
        out_shape=jax.ShapeDtypeStruct((n, SG_WIDTH), BF16),
        compiler_params=_cparams(("parallel",)),
        name="sg",
    )(proj, proj, g, ws, bst)


def _mix_kernel(ya_ref, yb_ref, yc_ref, ga_ref, gb_ref, gc_ref, wa_ref, wb_ref, wc_ref,
                wo_ref, x_ref, g_ref, o_ref, m_scr, *, nj, tn):
    j = pl.program_id(1)

    def branch(y_ref, w_ref, gate_ref):
        return _sigmoid(gate_ref[...].astype(F32)) * _dot(y_ref[...], w_ref[...])

    merged = branch(ya_ref, wa_ref, ga_ref) + branch(yb_ref, wb_ref, gb_ref) + branch(yc_ref, wc_ref, gc_ref)
    m_scr[j] = merged.astype(BF16)

    @pl.when(j == nj - 1)
    def _():
        acc = _dot(m_scr[0], wo_ref[0:tn, :])
        for jj in range(1, nj):
            acc = acc + _dot(m_scr[jj], wo_ref[jj * tn:(jj + 1) * tn, :])
        o_ref[...] = x_ref[...] + _rms(acc, g_ref[...])


def _mix_out(ya, yb, yc, proj, wa, wb, wc, wo, xf, g, tm=512, tn=512):
    n = xf.shape[0]
    nj = D_MODEL // tn
    yspec = pl.BlockSpec((tm, NA_WIDTH), lambda i, j: (i, 0))
    gate = lambda k: pl.BlockSpec((tm, tn), lambda i, j: (i, (P_G + k * D_MODEL) // tn + j))
    wspec = pl.BlockSpec((NA_WIDTH, tn), lambda i, j: (0, j))
    return pl.pallas_call(
        functools.partial(_mix_kernel, nj=nj, tn=tn),
        grid=(n // tm, nj),
        in_specs=[yspec, yspec, yspec, gate(0), gate(1), gate(2), wspec, wspec, wspec,
                  pl.BlockSpec((D_MODEL, D_MODEL), lambda i, j: (0, 0), pipeline_mode=pl.Buffered(1)),
                  pl.BlockSpec((tm, D_MODEL), lambda i, j: (i, 0)),
                  pl.BlockSpec((1, D_MODEL), lambda i, j: (0, 0))],
        out_specs=pl.BlockSpec((tm, D_MODEL), lambda i, j: (i, 0)),
        out_shape=jax.ShapeDtypeStruct((n, D_MODEL), F32),
        scratch_shapes=[pltpu.VMEM((nj, tm, tn), BF16)],
        compiler_params=_cparams(("parallel", "arbitrary")),
        name="mix_out",
    )(ya, yb, yc, proj, proj, proj, wa, wb, wc, wo, xf, g)


def _mlp_kernel(x_ref, g1_ref, wu_ref, wd_ref, g2_ref, o_ref, h_scr, acc_scr, *, nj):
    j = pl.program_id(1)

    @pl.when(j == 0)
    def _():
        h_scr[...] = _rms(x_ref[...], g1_ref[...]).astype(BF16)
        acc_scr[...] = jnp.zeros_like(acc_scr)

    a = jnp.maximum(_dot(h_scr[...], wu_ref[...]), 0.0)
    acc_scr[...] += _dot((a * a).astype(BF16), wd_ref[...])

    @pl.when(j == nj - 1)
    def _():
        o_ref[...] = x_ref[...] + _rms(acc_scr[...], g2_ref[...])


def _mlp(xf, g1, wu, wd, g2, tm=512, tf=1024):
    n = xf.shape[0]
    nj = D_FF // tf
    return pl.pallas_call(
        functools.partial(_mlp_kernel, nj=nj),
        grid=(n // tm, nj),
        in_specs=[pl.BlockSpec((tm, D_MODEL), lambda i, j: (i, 0)),
                  pl.BlockSpec((1, D_MODEL), lambda i, j: (0, 0)),
                  pl.BlockSpec((D_MODEL, tf), lambda i, j: (0, j)),
                  pl.BlockSpec((tf, D_MODEL), lambda i, j: (j, 0)),
                  pl.BlockSpec((1, D_MODEL), lambda i, j: (0, 0))],
        out_specs=pl.BlockSpec((tm, D_MODEL), lambda i, j: (i, 0)),
        out_shape=jax.ShapeDtypeStruct((n, D_MODEL), F32),
        scratch_shapes=[pltpu.VMEM((tm, D_MODEL), BF16), pltpu.VMEM((tm, D_MODEL), F32)],
        compiler_params=_cparams(("parallel", "arbitrary")),
        name="mlp",
    )(xf, g1, wu, wd, g2)


def _prep_layer(pre_mix_g, post_mix_g, pre_mlp_g, post_mlp_g, w_in, b_gate, conv_w, na_rpb,
                ml_norm_g, sg_norm_g, w_s, b_s, w_a, w_b, w_c, w_out, w_up, w_down):
    w_main = jnp.concatenate([w_in[:, :OFF_BG], w_in[:, OFF_C:]], axis=1).astype(BF16)
    w_g = w_in[:, OFF_BG:OFF_C].astype(BF16)
    w_gate = jnp.pad(w_g, ((0, 0), (0, LANE - N_GATE)))
    bg_col = jnp.pad(b_gate.astype(F32), (0, LANE - N_GATE)).reshape(1, LANE)
    return dict(
        pre_mix_g=pre_mix_g.reshape(1, D_MODEL), post_mix_g=post_mix_g.reshape(1, D_MODEL),
        pre_mlp_g=pre_mlp_g.reshape(1, D_MODEL), post_mlp_g=post_mlp_g.reshape(1, D_MODEL),
        w_main=w_main, w_gate=w_gate, w_gate_t=w_g.T, bg_col=bg_col,
        bg_row=b_gate.astype(F32).reshape(N_GATE, 1),
        conv_w=conv_w.astype(F32), bias_tab=_na_bias_table(na_rpb),
        ml_norm_g=ml_norm_g.reshape(1, ML_WIDTH), sg_norm_g=sg_norm_g.reshape(1, SG_WIDTH),
        w_s=w_s.astype(BF16), bst=b_s.astype(F32).T,
        w_a=w_a.astype(BF16), w_b=w_b.astype(BF16), w_c=w_c.astype(BF16),
        w_out=w_out.astype(BF16), w_up=w_up.astype(BF16), w_down=w_down.astype(BF16),
    )


def _trunk_layer(x, p):
    b, t, _ = x.shape
    n = b * t
    xf = x.reshape(n, D_MODEL)
    proj, gate_col, gate_row = _in_proj(xf, p["pre_mix_g"], p["w_main"], p["w_gate"], p["w_gate_t"],
                                        p["bg_col"], p["bg_row"])
    proj3 = proj.reshape(b, t, P_WIDTH)
    y_a = _na(proj3, p["bias_tab"]).reshape(n, NA_WIDTH)
    h_f, qs, ks = _mlstm_fwd(proj3, gate_col, gate_row, p["conv_w"])
    y_b = _mlstm_bwd(qs, ks, proj3, gate_col, gate_row, h_f, p["ml_norm_g"]).reshape(n, ML_WIDTH)
    y_c = _sg(proj, p["sg_norm_g"], p["w_s"], p["bst"])
    x1 = _mix_out(y_a, y_b, y_c, proj, p["w_a"], p["w_b"], p["w_c"], p["w_out"], xf, p["post_mix_g"])
    x2 = _mlp(x1, p["pre_mlp_g"], p["w_up"], p["w_down"], p["post_mlp_g"])
    return x2.reshape(b, t, D_MODEL)


def kernel(x_prompt, x_sample, pre_mix_g, post_mix_g, pre_mlp_g, post_mlp_g, w_in, b_gate, conv_w,
           na_rpb, ml_norm_g, sg_norm_g, w_s, b_s, w_a, w_b, w_c, w_out, w_up, w_down):
    depth = w_in.shape[0]
    layers = [
        _prep_layer(pre_mix_g[l], post_mix_g[l], pre_mlp_g[l], post_mlp_g[l], w_in[l], b_gate[l],
                    conv_w[l], na_rpb[l], ml_norm_g[l], sg_norm_g[l], w_s[l], b_s[l], w_a[l], w_b[l],
                    w_c[l], w_out[l], w_up[l], w_down[l])
        for l in range(depth)
    ]

    def run(x):
        for p in layers:
            x = _trunk_layer(x, p)
        return x

    return (run(x_prompt), run(x_sample))
```

```python
import functools

import numpy as np
import jax
import jax.numpy as jnp
from jax import lax
from jax.experimental import pallas as pl
from jax.experimental.pallas import tpu as pltpu

F32 = jnp.float32
BF16 = jnp.bfloat16

D_MODEL = 2048
GRID_W = 64
EPS = 1e-6

NA_HEADS = 8
NA_HEAD_DIM = 128
NA_WIDTH = NA_HEADS * NA_HEAD_DIM
NA_KR = 8
NA_KC = 16
NA_ROWS_PER_STEP = 8

ML_HEADS = 4
ML_HEAD_DIM = 256
ML_WIDTH = ML_HEADS * ML_HEAD_DIM
ML_CHUNK = 256
ML_STRIP = 256
MIX_TN = 512

SG_GROUPS = 8
SG_CHUNK = 128
SG_WIDTH = 1024
SG_GROUP_DIM = SG_WIDTH // SG_GROUPS

D_FF = 4 * D_MODEL
N_BRANCH = 3

OFF_A = 0
OFF_B = OFF_A + 3 * NA_WIDTH
OFF_BG = OFF_B + 4 * ML_WIDTH
OFF_C = OFF_BG + 4 * ML_HEADS
OFF_G = OFF_C + 2 * SG_WIDTH
D_IN = OFF_G + N_BRANCH * D_MODEL
N_GATE = 4 * ML_HEADS

P_A = 0
P_B = P_A + 3 * NA_WIDTH
P_C = P_B + 4 * ML_WIDTH
P_G = P_C + 2 * SG_WIDTH
P_WIDTH = P_G + N_BRANCH * D_MODEL

LANE = 128
HALO_ROWS = 16
VMEM_LIMIT = 56 * 1024 * 1024
LOG2E = np.float32(1.4426950408889634)


def _cparams(sem, vmem=VMEM_LIMIT, **kw):
    return pltpu.CompilerParams(dimension_semantics=sem, vmem_limit_bytes=vmem, **kw)


def _rms(x, g):
    return x * lax.rsqrt(jnp.mean(x * x, axis=-1, keepdims=True) + EPS) * g


def _sigmoid(x):
    return 1.0 / (1.0 + jnp.exp(-x))


def _log_sigmoid(x):
    return jnp.minimum(x, 0.0) - jnp.log(1.0 + jnp.exp(-jnp.abs(x)))


def _gelu_tanh(x):
    c = np.float32(np.sqrt(2.0 / np.pi))
    return 0.5 * x * (1.0 + jnp.tanh(c * (x + 0.044715 * (x * x * x))))


def _dot(a, b):
    return jnp.dot(a, b, preferred_element_type=F32)


def _dot_nt(a, b):
    return lax.dot_general(a, b, (((1,), (1,)), ((), ())), preferred_element_type=F32)


def _dot_tn(a, b):
    return lax.dot_general(a, b, (((0,), (0,)), ((), ())), preferred_element_type=F32)


def _inproj_kernel(x_ref, g_ref, w_ref, wg_ref, wgt_ref, bgc_ref, bgr_ref,
                   o_ref, gc_ref, gr_ref, h_scr):
    @pl.when(pl.program_id(1) == 0)
    def _():
        hb = _rms(x_ref[...], g_ref[...]).astype(BF16)
        h_scr[...] = hb
        gc_ref[...] = _dot(hb, wg_ref[...]) + bgc_ref[...]
        gr_ref[...] = _dot_nt(wgt_ref[...], hb) + bgr_ref[...]

    o_ref[...] = _dot(h_scr[...], w_ref[...]).astype(BF16)


def _in_proj(xf, g, w_main, w_gate, w_gate_t, bg_col, bg_row, tm=1024, tn=1024):
    n = xf.shape[0]
    tm = min(tm, n)
    grid = (n // tm, P_WIDTH // tn)
    return pl.pallas_call(
        _inproj_kernel,
        grid=grid,
        in_specs=[
            pl.BlockSpec((tm, D_MODEL), lambda i, j: (i, 0)),
            pl.BlockSpec((1, D_MODEL), lambda i, j: (0, 0)),
            pl.BlockSpec((D_MODEL, tn), lambda i, j: (0, j)),
            pl.BlockSpec((D_MODEL, LANE), lambda i, j: (0, 0)),
            pl.BlockSpec((N_GATE, D_MODEL), lambda i, j: (0, 0)),
            pl.BlockSpec((1, LANE), lambda i, j: (0, 0)),
            pl.BlockSpec((N_GATE, 1), lambda i, j: (0, 0)),
        ],
        out_specs=[
            pl.BlockSpec((tm, tn), lambda i, j: (i, j)),
            pl.BlockSpec((tm, LANE), lambda i, j: (i, 0)),
            pl.BlockSpec((N_GATE, tm), lambda i, j: (0, i)),
        ],
        out_shape=[
            jax.ShapeDtypeStruct((n, P_WIDTH), BF16),
            jax.ShapeDtypeStruct((n, LANE), F32),
            jax.ShapeDtypeStruct((N_GATE, n), F32),
        ],
        scratch_shapes=[pltpu.VMEM((tm, D_MODEL), BF16)],
        compiler_params=_cparams(("parallel", "arbitrary")),
        name="in_proj",
    )(xf, g, w_main, w_gate, w_gate_t, bg_col, bg_row)


def _na_bias_table(rpb):
    qc = np.arange(GRID_W)[:, None]
    kc = np.arange(GRID_W)[None, :]
    cs = np.clip(qc - NA_KC // 2, 0, GRID_W - NA_KC)
    ok = (kc >= cs) & (kc < cs + NA_KC)
    dc = np.clip(kc - qc + NA_KC - 1, 0, 2 * NA_KC - 2)
    d = np.arange(NA_KR)[:, None] + np.arange(NA_KR)[None, :]
    tab = rpb.astype(F32)[:, d][:, :, :, dc]
    tab = jnp.where(jnp.asarray(ok)[None, None, None], tab, -1e30)
    tab = tab.transpose(0, 1, 3, 2, 4)
    return tab.reshape(NA_HEADS, NA_KR, GRID_W, NA_KR * GRID_W)


def _na_kernel(q_ref, k_ref, v_ref, bias_ref, o_ref, *, rows):
    scale = np.float32(NA_HEAD_DIM ** -0.5)
    win = NA_KR * GRID_W

    def body(i, carry):
        rr = [i * NA_ROWS_PER_STEP + u for u in range(NA_ROWS_PER_STEP)]
        rs = [jnp.clip(r - NA_KR // 2, 0, rows - NA_KR) for r in rr]
        q0 = [pl.multiple_of(r * GRID_W, GRID_W) for r in rr]
        k0 = [pl.multiple_of(x * GRID_W, GRID_W) for x in rs]
        s = [_dot_nt(q_ref[pl.ds(q0[u], GRID_W), :], k_ref[pl.ds(k0[u], win), :])
             for u in range(NA_ROWS_PER_STEP)]
        p, inv = [], []
        for u in range(NA_ROWS_PER_STEP):
            su = s[u] * scale + bias_ref[rs[u] - rr[u] + NA_KR - 1]
            e = jnp.exp(su - jnp.max(su, axis=-1, keepdims=True))
            inv.append(1.0 / jnp.sum(e, axis=-1, keepdims=True))
            p.append(e.astype(BF16))
        for u in range(NA_ROWS_PER_STEP):
            o = _dot(p[u], v_ref[pl.ds(k0[u], win), :]) * inv[u]
            o_ref[pl.ds(q0[u], GRID_W), :] = o.astype(BF16)
        return carry

    lax.fori_loop(0, rows // NA_ROWS_PER_STEP, body, 0)


def _na(proj3, bias_tab):
    b, t, _ = proj3.shape
    rows = t // GRID_W
    blk = lambda off: pl.BlockSpec((None, t, NA_HEAD_DIM), lambda bi, h: (bi, 0, off + h))
    return pl.pallas_call(
        functools.partial(_na_kernel, rows=rows),
        grid=(b, NA_HEADS),
        in_specs=[
            blk(P_A // NA_HEAD_DIM),
            blk(P_A // NA_HEAD_DIM + NA_HEADS),
            blk(P_A // NA_HEAD_DIM + 2 * NA_HEADS),
            pl.BlockSpec((None, NA_KR, GRID_W, NA_KR * GRID_W), lambda bi, h: (h, 0, 0, 0)),
        ],
        out_specs=pl.BlockSpec((None, t, NA_HEAD_DIM), lambda bi, h: (bi, 0, h)),
        out_shape=jax.ShapeDtypeStruct((b, t, NA_WIDTH), BF16),
        compiler_params=_cparams(("parallel", "parallel")),
        name="na",
    )(proj3, proj3, proj3, bias_tab)


def _tri_product(tri_b, x, *, left):
    hi = x.astype(BF16)
    rest = x - hi.astype(F32)
    mid = rest.astype(BF16)
    lo = (rest - mid.astype(F32)).astype(BF16)
    if left:
        return (_dot(tri_b, lo) + _dot(tri_b, mid)) + _dot(tri_b, hi)
    return (_dot(lo, tri_b) + _dot(mid, tri_b)) + _dot(hi, tri_b)


def _mlstm_init(c_scr, n_scr, m_scr):
    @pl.when(pl.program_id(1) == 0)
    def _():
        c_scr[...] = jnp.zeros_like(c_scr)
        n_scr[...] = jnp.zeros_like(n_scr)
        m_scr[...] = jnp.zeros_like(m_scr)


def _mlstm_chunk(q_of, k_of, v_ref, gc_ref, gr_ref, c_scr, n_scr, m_scr, emit, *, reverse, L):
    ri = lax.broadcasted_iota(jnp.int32, (L, L), 0)
    ci = lax.broadcasted_iota(jnp.int32, (L, L), 1)
    mask = (ci >= ri) if reverse else (ci <= ri)
    mask_t = (ri >= ci) if reverse else (ri <= ci)
    gi = 2 if reverse else 0
    last = 0 if reverse else L - 1
    R = ML_STRIP

    gcol = gc_ref[...]
    grow = gr_ref[...]
    b_col_all = _tri_product(jnp.where(mask, 1.0, 0.0).astype(BF16), _log_sigmoid(gcol), left=True)
    b_row_all = _tri_product(jnp.where(mask_t, 1.0, 0.0).astype(BF16), _log_sigmoid(grow), left=False)
    b2_col_all = b_col_all * LOG2E

    heads = range(ML_HEADS)
    hsl = [slice(hd * ML_HEAD_DIM, (hd + 1) * ML_HEAD_DIM) for hd in heads]
    ii = [gi * ML_HEADS + hd for hd in heads]
    fi = [(gi + 1) * ML_HEADS + hd for hd in heads]
    m_prev = [m_scr[hd][:, 0:1] for hd in heads]
    c_prev = [c_scr[hd] for hd in heads]
    n_prev = [n_scr[hd] for hd in heads]
    q = [q_of(hd) for hd in heads]
    k = [k_of(hd) for hd in heads]
    vb = [v_ref[:, hsl[hd]] for hd in heads]

    strips = []
    for st in range(L // R):
        if reverse:
            c0, c1 = (st * R) // LANE * LANE, L
        else:
            c0, c1 = 0, -(-((st + 1) * R) // LANE) * LANE
        strips.append((st, slice(st * R, (st + 1) * R), slice(c0, c1)))
    units = [(hd, st, rows, cols) for hd in heads for (st, rows, cols) in strips]

    qk = [_dot_nt(q[hd][1][rows], k[hd][1][cols]) for (hd, st, rows, cols) in units]
    qc = [_dot(q[hd][1][rows], c_prev[hd].astype(BF16)) for (hd, st, rows, cols) in units]

    sb, scale_c, inv = [], [], []
    for u, (hd, st, rows, cols) in enumerate(units):
        c0, c1 = cols.start, cols.stop
        b2_col = b2_col_all[rows, fi[hd]:fi[hd] + 1]
        r2_row = (grow[ii[hd]:ii[hd] + 1, cols] - b_row_all[fi[hd]:fi[hd] + 1, cols]) * LOG2E
        sri = lax.broadcasted_iota(jnp.int32, (R, c1 - c0), 0) + st * R
        sci = lax.broadcasted_iota(jnp.int32, (R, c1 - c0), 1) + c0
        smask = (sci >= sri) if reverse else (sci <= sri)
        d2 = jnp.where(smask, b2_col + r2_row, -1e30)
        inter2 = b2_col + m_prev[hd] * LOG2E
        m2 = jnp.maximum(inter2, jnp.max(d2, axis=1, keepdims=True))
        s = qk[u] * jnp.exp2(d2 - m2)
        w_inter = jnp.exp2(inter2 - m2)
        nq = (jnp.sum(s, axis=1, keepdims=True)
              + w_inter * jnp.sum(q[hd][0][rows] * n_prev[hd], axis=1, keepdims=True))
        r = 1.0 / jnp.maximum(jnp.abs(nq), jnp.exp2(-m2))
        sb.append(s.astype(BF16))
        scale_c.append(w_inter * r)
        inv.append(r)

    for u, (hd, st, rows, cols) in enumerate(units):
        emit(hd, rows, _dot(sb[u], vb[hd][cols]) * inv[u] + qc[u] * scale_c[u])

    kws, decays, m_news = [], [], []
    for hd in heads:
        b_col = b_col_all[:, fi[hd]:fi[hd] + 1]
        b_last = b_col_all[last:last + 1, fi[hd]:fi[hd] + 1]
        wlog = b_last - b_col + gcol[:, ii[hd]:ii[hd] + 1]
        m_new = jnp.maximum(b_last + m_prev[hd], jnp.max(wlog, axis=0, keepdims=True))
        kws.append(k[hd][0] * jnp.exp(wlog - m_new))
        decays.append(jnp.exp(b_last + m_prev[hd] - m_new))
        m_news.append(m_new)
    eye = jnp.where(lax.broadcasted_iota(jnp.int32, (ML_HEAD_DIM, ML_HEAD_DIM), 0)
                    == lax.broadcasted_iota(jnp.int32, (ML_HEAD_DIM, ML_HEAD_DIM), 1), 1.0, 0.0).astype(BF16)
    kwt = [_dot_nt(eye, kws[hd].astype(BF16)).astype(BF16) for hd in heads]
    for hd in heads:
        c_scr[hd] = decays[hd] * c_prev[hd] + _dot(kwt[hd], vb[hd])
        n_scr[hd] = decays[hd] * n_prev[hd] + jnp.sum(kws[hd], axis=0, keepdims=True)
        m_scr[hd] = jnp.broadcast_to(m_news[hd], (1, LANE))


def _mlstm_fwd_kernel(qp_ref, q_ref, qn_ref, kp_ref, k_ref, kn_ref, v_ref, gc_ref, gr_ref, cw_ref,
                      h_ref, qs_ref, ks_ref, c_scr, n_scr, m_scr, *, nc, L):
    c = pl.program_id(1)
    _mlstm_init(c_scr, n_scr, m_scr)

    rowi = lax.broadcasted_iota(jnp.int32, (L, 1), 0)
    has_prev = (c > 0).astype(F32)
    has_next = (c < nc - 1).astype(F32)

    def conv_silu(p_ref, x_ref, n_ref, w):
        x = x_ref[...].astype(F32)
        prev_row = p_ref[HALO_ROWS - 1:HALO_ROWS, :].astype(F32) * has_prev
        next_row = n_ref[0:1, :].astype(F32) * has_next
        xp = jnp.where(rowi == 0, prev_row, pltpu.roll(x, 1, 0))
        xn = jnp.where(rowi == L - 1, next_row, pltpu.roll(x, L - 1, 0))
        y = xp * w[0:1] + x * w[1:2] + xn * w[2:3]
        return y * _sigmoid(y)

    cw = cw_ref[...]
    qs = conv_silu(qp_ref, q_ref, qn_ref, cw[:, :ML_WIDTH]) * np.float32(ML_HEAD_DIM ** -0.5)
    ks = conv_silu(kp_ref, k_ref, kn_ref, cw[:, ML_WIDTH:])
    qsb = qs.astype(BF16)
    ksb = ks.astype(BF16)
    qs_ref[...] = qsb
    ks_ref[...] = ksb

    def head(x, xb):
        return lambda hd: (x[:, hd * ML_HEAD_DIM:(hd + 1) * ML_HEAD_DIM],
                           xb[:, hd * ML_HEAD_DIM:(hd + 1) * ML_HEAD_DIM])

    def emit(hd, rows, h):
        h_ref[rows, hd * ML_HEAD_DIM:(hd + 1) * ML_HEAD_DIM] = h

    _mlstm_chunk(head(qs, qsb), head(ks, ksb), v_ref, gc_ref, gr_ref, c_scr, n_scr, m_scr, emit,
                 reverse=False, L=L)


def _mlstm_bwd_kernel(qs_ref, ks_ref, v_ref, gc_ref, gr_ref, hf_ref, o_ref, g_ref,
                      y_ref, c_scr, n_scr, m_scr, *, L):
    _mlstm_init(c_scr, n_scr, m_scr)

    def head(x_ref):
        def get(hd):
            xb = x_ref[:, hd * ML_HEAD_DIM:(hd + 1) * ML_HEAD_DIM]
            return xb.astype(F32), xb
        return get

    def emit(hd, rows, h):
        sl = slice(hd * ML_HEAD_DIM, (hd + 1) * ML_HEAD_DIM)
        y = _rms(h + hf_ref[rows, sl], g_ref[:, sl]) * _sigmoid(o_ref[rows, sl].astype(F32))
        y_ref[rows, sl] = y.astype(BF16)

    _mlstm_chunk(head(qs_ref), head(ks_ref), v_ref, gc_ref, gr_ref, c_scr, n_scr, m_scr, emit,
                 reverse=True, L=L)


def _mlstm_scratch():
    return [pltpu.VMEM((ML_HEADS, ML_HEAD_DIM, ML_HEAD_DIM), F32),
            pltpu.VMEM((ML_HEADS, 1, ML_HEAD_DIM), F32),
            pltpu.VMEM((ML_HEADS, 1, LANE), F32)]


def _mlstm_fwd(proj3, gate_col, gate_row, conv_w):
    b, t, _ = proj3.shape
    L = ML_CHUNK
    nc = t // L
    hb = L // HALO_ROWS
    n_halo = t // HALO_ROWS
    cur = lambda col: pl.BlockSpec((None, L, ML_WIDTH), lambda bi, c: (bi, c, col))
    prv = lambda col: pl.BlockSpec((None, HALO_ROWS, ML_WIDTH),
                                   lambda bi, c: (bi, jnp.maximum(c * hb - 1, 0), col))
    nxt = lambda col: pl.BlockSpec((None, HALO_ROWS, ML_WIDTH),
                                   lambda bi, c: (bi, jnp.minimum((c + 1) * hb, n_halo - 1), col))
    qc, kc, vc = P_B // ML_WIDTH, P_B // ML_WIDTH + 1, P_B // ML_WIDTH + 2
    return pl.pallas_call(
        functools.partial(_mlstm_fwd_kernel, nc=nc, L=L),
        grid=(b, nc),
        in_specs=[
            prv(qc), cur(qc), nxt(qc),
            prv(kc), cur(kc), nxt(kc),
            cur(vc),
            pl.BlockSpec((L, LANE), lambda bi, c: (bi * nc + c, 0)),
            pl.BlockSpec((N_GATE, L), lambda bi, c: (0, bi * nc + c)),
            pl.BlockSpec((3, 2 * ML_WIDTH), lambda bi, c: (0, 0)),
        ],
        out_specs=[cur(0), cur(0), cur(0)],
        out_shape=[jax.ShapeDtypeStruct((b, t, ML_WIDTH), F32),
                   jax.ShapeDtypeStruct((b, t, ML_WIDTH), BF16),
                   jax.ShapeDtypeStruct((b, t, ML_WIDTH), BF16)],
        scratch_shapes=_mlstm_scratch(),
        compiler_params=_cparams(("parallel", "arbitrary")),
        name="mlstm_fwd",
    )(proj3, proj3, proj3, proj3, proj3, proj3, proj3, gate_col, gate_row, conv_w)


def _mlstm_bwd(qs, ks, proj3, gate_col, gate_row, h_f, g):
    b, t, _ = proj3.shape
    L = ML_CHUNK
    nc = t // L
    cur = lambda col: pl.BlockSpec((None, L, ML_WIDTH), lambda bi, c: (bi, nc - 1 - c, col))
    vc, oc = P_B // ML_WIDTH + 2, P_B // ML_WIDTH + 3
    return pl.pallas_call(
        functools.partial(_mlstm_bwd_kernel, L=L),
        grid=(b, nc),
        in_specs=[
            cur(0), cur(0), cur(vc),
            pl.BlockSpec((L, LANE), lambda bi, c: (bi * nc + nc - 1 - c, 0)),
            pl.BlockSpec((N_GATE, L), lambda bi, c: (0, bi * nc + nc - 1 - c)),
            cur(0), cur(oc),
            pl.BlockSpec((1, ML_WIDTH), lambda bi, c: (0, 0)),
        ],
        out_specs=cur(0),
        out_shape=jax.ShapeDtypeStruct((b, t, ML_WIDTH), BF16),
        scratch_shapes=_mlstm_scratch(),
        compiler_params=_cparams(("parallel", "arbitrary")),
        name="mlstm_bwd",
    )(qs, ks, proj3, gate_col, gate_row, h_f, proj3, g)


def _sg_kernel(u_ref, v_ref, g_ref, ws_ref, bst_ref, y_ref, *, chunks):
    for cidx in range(chunks):
        rs = slice(cidx * SG_CHUNK, (cidx + 1) * SG_CHUNK)
        zu = _gelu_tanh(u_ref[rs, :].astype(F32))
        zv = _gelu_tanh(v_ref[rs, :].astype(F32))
        vb = _rms(zv, g_ref[...]).astype(BF16)
        for gidx in range(SG_GROUPS):
            cs = slice(gidx * SG_GROUP_DIM, (gidx + 1) * SG_GROUP_DIM)
            mixed = _dot(ws_ref[gidx], vb[:, cs]) + bst_ref[:, gidx:gidx + 1]
            y_ref[rs, cs] = (zu[:, cs] * mixed).astype(BF16)


def _sg(proj, g, ws, bst, chunks=4):
    n = proj.shape[0]
    tm = chunks * SG_CHUNK
    row = lambda col: pl.BlockSpec((tm, SG_WIDTH), lambda i: (i, col))
    return pl.pallas_call(
        functools.partial(_sg_kernel, chunks=chunks),
        grid=(n // tm,),
        in_specs=[row(P_C // SG_WIDTH), row(P_C // SG_WIDTH + 1),
                  pl.BlockSpec((1, SG_WIDTH), lambda i: (0, 0)),
                  pl.BlockSpec((SG_GROUPS, SG_CHUNK, SG_CHUNK), lambda i: (0, 0, 0)),
                  pl.BlockSpec((SG_CHUNK, SG_GROUPS), lambda i: (0, 0))],
        out_specs=row(0),
        out_shape=jax.ShapeDtypeStruct((n, SG_WIDTH), BF16),
        compiler_params=_cparams(("parallel",)),
        name="sg",
    )(proj, proj, g, ws, bst)


def _mix_kernel(ya_ref, yb_ref, yc_ref, ga_ref, gb_ref, gc_ref, wa_ref, wb_ref, wc_ref,
                wo_ref, x_ref, g_ref, o_ref, m_scr, *, nj, tn):
    j = pl.program_id(1)

    def branch(y_ref, w_ref, gate_ref):
        return _sigmoid(gate_ref[...].astype(F32)) * _dot(y_ref[...], w_ref[j])

    merged = branch(ya_ref, wa_ref, ga_ref) + branch(yb_ref, wb_ref, gb_ref) + branch(yc_ref, wc_ref, gc_ref)
    m_scr[j] = merged.astype(BF16)

    @pl.when(j == nj - 1)
    def _():
        acc = _dot(m_scr[0], wo_ref[0:tn, :])
        for jj in range(1, nj):
            acc = acc + _dot(m_scr[jj], wo_ref[jj * tn:(jj + 1) * tn, :])
        o_ref[...] = x_ref[...] + _rms(acc, g_ref[...])


def _mix_out(ya, yb, yc, proj, wa, wb, wc, wo, xf, g, tm=512):
    n = xf.shape[0]
    nj, _, tn = wa.shape
    yspec = pl.BlockSpec((tm, NA_WIDTH), lambda i, j: (i, 0))
    gate = lambda k: pl.BlockSpec((tm, tn), lambda i, j: (i, (P_G + k * D_MODEL) // tn + j))
    wspec = pl.BlockSpec((nj, NA_WIDTH, tn), lambda i, j: (0, 0, 0), pipeline_mode=pl.Buffered(1))
    return pl.pallas_call(
        functools.partial(_mix_kernel, nj=nj, tn=tn),
        grid=(n // tm, nj),
        in_specs=[yspec, yspec, yspec, gate(0), gate(1), gate(2), wspec, wspec, wspec,
                  pl.BlockSpec((D_MODEL, D_MODEL), lambda i, j: (0, 0), pipeline_mode=pl.Buffered(1)),
                  pl.BlockSpec((tm, D_MODEL), lambda i, j: (i, 0)),
                  pl.BlockSpec((1, D_MODEL), lambda i, j: (0, 0))],
        out_specs=pl.BlockSpec((tm, D_MODEL), lambda i, j: (i, 0)),
        out_shape=jax.ShapeDtypeStruct((n, D_MODEL), F32),
        scratch_shapes=[pltpu.VMEM((nj, tm, tn), BF16)],
        compiler_params=_cparams(("parallel", "arbitrary")),
        name="mix_out",
    )(ya, yb, yc, proj, proj, proj, wa, wb, wc, wo, xf, g)


def _mlp_kernel(x_ref, g1_ref, wu_ref, wd_ref, g2_ref, o_ref, h_scr, acc_scr, *, nj):
    j = pl.program_id(1)

    @pl.when(j == 0)
    def _():
        h_scr[...] = _rms(x_ref[...], g1_ref[...]).astype(BF16)
        acc_scr[...] = jnp.zeros_like(acc_scr)

    a = jnp.maximum(_dot(h_scr[...], wu_ref[...]), 0.0)
    acc_scr[...] += _dot((a * a).astype(BF16), wd_ref[...])

    @pl.when(j == nj - 1)
    def _():
        o_ref[...] = x_ref[...] + _rms(acc_scr[...], g2_ref[...])


def _mlp(xf, g1, wu, wd, g2, tm=512, tf=1024):
    n = xf.shape[0]
    nj = D_FF // tf
    return pl.pallas_call(
        functools.partial(_mlp_kernel, nj=nj),
        grid=(n // tm, nj),
        in_specs=[pl.BlockSpec((tm, D_MODEL), lambda i, j: (i, 0)),
                  pl.BlockSpec((1, D_MODEL), lambda i, j: (0, 0)),
                  pl.BlockSpec((D_MODEL, tf), lambda i, j: (0, j)),
                  pl.BlockSpec((tf, D_MODEL), lambda i, j: (j, 0)),
                  pl.BlockSpec((1, D_MODEL), lambda i, j: (0, 0))],
        out_specs=pl.BlockSpec((tm, D_MODEL), lambda i, j: (i, 0)),
        out_shape=jax.ShapeDtypeStruct((n, D_MODEL), F32),
        scratch_shapes=[pltpu.VMEM((tm, D_MODEL), BF16), pltpu.VMEM((tm, D_MODEL), F32)],
        compiler_params=_cparams(("parallel", "arbitrary")),
        name="mlp",
    )(xf, g1, wu, wd, g2)


def _col_tiles(w, tn=MIX_TN):
    k, n = w.shape
    return w.astype(BF16).reshape(k, n // tn, tn).transpose(1, 0, 2)


def _prep_layer(pre_mix_g, post_mix_g, pre_mlp_g, post_mlp_g, w_in, b_gate, conv_w, na_rpb,
                ml_norm_g, sg_norm_g, w_s, b_s, w_a, w_b, w_c, w_out, w_up, w_down):
    w_main = jnp.concatenate([w_in[:, :OFF_BG], w_in[:, OFF_C:]], axis=1).astype(BF16)
    w_g = w_in[:, OFF_BG:OFF_C].astype(BF16)
    w_gate = jnp.pad(w_g, ((0, 0), (0, LANE - N_GATE)))
    bg_col = jnp.pad(b_gate.astype(F32), (0, LANE - N_GATE)).reshape(1, LANE)
    return dict(
        pre_mix_g=pre_mix_g.reshape(1, D_MODEL), post_mix_g=post_mix_g.reshape(1, D_MODEL),
        pre_mlp_g=pre_mlp_g.reshape(1, D_MODEL), post_mlp_g=post_mlp_g.reshape(1, D_MODEL),
        w_main=w_main, w_gate=w_gate, w_gate_t=w_g.T, bg_col=bg_col,
        bg_row=b_gate.astype(F32).reshape(N_GATE, 1),
        conv_w=conv_w.astype(F32), bias_tab=_na_bias_table(na_rpb),
        ml_norm_g=ml_norm_g.reshape(1, ML_WIDTH), sg_norm_g=sg_norm_g.reshape(1, SG_WIDTH),
        w_s=w_s.astype(BF16), bst=b_s.astype(F32).T,
        w_a=_col_tiles(w_a), w_b=_col_tiles(w_b), w_c=_col_tiles(w_c),
        w_out=w_out.astype(BF16), w_up=w_up.astype(BF16), w_down=w_down.astype(BF16),
    )


def _trunk_layer(x, p):
    b, t, _ = x.shape
    n = b * t
    xf = x.reshape(n, D_MODEL)
    proj, gate_col, gate_row = _in_proj(xf, p["pre_mix_g"], p["w_main"], p["w_gate"], p["w_gate_t"],
                                        p["bg_col"], p["bg_row"])
    proj3 = proj.reshape(b, t, P_WIDTH)
    y_a = _na(proj3, p["bias_tab"]).reshape(n, NA_WIDTH)
    h_f, qs, ks = _mlstm_fwd(proj3, gate_col, gate_row, p["conv_w"])
    y_b = _mlstm_bwd(qs, ks, proj3, gate_col, gate_row, h_f, p["ml_norm_g"]).reshape(n, ML_WIDTH)
    y_c = _sg(proj, p["sg_norm_g"], p["w_s"], p["bst"])
    x1 = _mix_out(y_a, y_b, y_c, proj, p["w_a"], p["w_b"], p["w_c"], p["w_out"], xf, p["post_mix_g"])
    x2 = _mlp(x1, p["pre_mlp_g"], p["w_up"], p["w_down"], p["post_mlp_g"])
    return x2.reshape(b, t, D_MODEL)


def kernel(x_prompt, x_sample, pre_mix_g, post_mix_g, pre_mlp_g, post_mlp_g, w_in, b_gate, conv_w,
           na_rpb, ml_norm_g, sg_norm_g, w_s, b_s, w_a, w_b, w_c, w_out, w_up, w_down):
    depth = w_in.shape[0]
    layers = [
        _prep_layer(pre_mix_g[l], post_mix_g[l], pre_mlp_g[l], post_mlp_g[l], w_in[l], b_gate[l],
                    conv_w[l], na_rpb[l], ml_norm_g[l], sg_norm_g[l], w_s[l], b_s[l], w_a[l], w_b[l],
                    w_c[l], w_out[l], w_up[l], w_down[l])
        for l in range(depth)
    ]

    def run(x):
        for p in layers:
            x = _trunk_layer(x, p)
        return x

    return (run(x_prompt), run(x_sample))
```

```python
import functools
import math

import numpy as np
import jax
import jax.numpy as jnp
from jax import lax
from jax.experimental import pallas as pl
from jax.experimental.pallas import tpu as pltpu

F32 = jnp.float32
BF16 = jnp.bfloat16

D_MODEL = 2048
GRID_W = 64
EPS = 1e-6

NA_HEADS = 8
NA_HEAD_DIM = 128
NA_WIDTH = NA_HEADS * NA_HEAD_DIM
NA_KR = 8
NA_KC = 16
NA_ROWS_PER_STEP = 16

ML_HEADS = 4
ML_HEAD_DIM = 256
ML_WIDTH = ML_HEADS * ML_HEAD_DIM
ML_CHUNK = 256
ML_STRIP = 256
MIX_TN = 512

SG_GROUPS = 8
SG_CHUNK = 128
SG_WIDTH = 1024
SG_GROUP_DIM = SG_WIDTH // SG_GROUPS

D_FF = 4 * D_MODEL
N_BRANCH = 3

OFF_A = 0
OFF_B = OFF_A + 3 * NA_WIDTH
OFF_BG = OFF_B + 4 * ML_WIDTH
OFF_C = OFF_BG + 4 * ML_HEADS
OFF_G = OFF_C + 2 * SG_WIDTH
D_IN = OFF_G + N_BRANCH * D_MODEL
N_GATE = 4 * ML_HEADS

P_A = 0
P_B = P_A + 3 * NA_WIDTH
P_C = P_B + 4 * ML_WIDTH
P_G = P_C + 2 * SG_WIDTH
P_WIDTH = P_G + N_BRANCH * D_MODEL

LANE = 128
HALO_ROWS = 16
VMEM_LIMIT = 56 * 1024 * 1024
LOG2E = np.float32(1.4426950408889634)


def _cparams(sem, vmem=VMEM_LIMIT, **kw):
    return pltpu.CompilerParams(dimension_semantics=sem, vmem_limit_bytes=vmem, **kw)


def _rms(x, g):
    return x * lax.rsqrt(jnp.mean(x * x, axis=-1, keepdims=True) + EPS) * g


def _sigmoid(x):
    return 1.0 / (1.0 + jnp.exp(-x))


def _log_sigmoid(x):
    return jnp.minimum(x, 0.0) - jnp.log(1.0 + jnp.exp(-jnp.abs(x)))


def _gelu_tanh(x):
    c = np.float32(np.sqrt(2.0 / np.pi))
    return 0.5 * x * (1.0 + jnp.tanh(c * (x + 0.044715 * (x * x * x))))


def _dot(a, b):
    return jnp.dot(a, b, preferred_element_type=F32)


def _dot_nt(a, b):
    return lax.dot_general(a, b, (((1,), (1,)), ((), ())), preferred_element_type=F32)


def _dot_tn(a, b):
    return lax.dot_general(a, b, (((0,), (0,)), ((), ())), preferred_element_type=F32)


def _inproj_kernel(x_ref, g_ref, w_ref, wg_ref, wgt_ref, bgc_ref, bgr_ref,
                   o_ref, gc_ref, gr_ref, h_scr):
    @pl.when(pl.program_id(1) == 0)
    def _():
        hb = _rms(x_ref[...], g_ref[...]).astype(BF16)
        h_scr[...] = hb
        gc_ref[...] = _dot(hb, wg_ref[...]) + bgc_ref[...]
        gr_ref[...] = _dot_nt(wgt_ref[...], hb) + bgr_ref[...]

    o_ref[...] = _dot(h_scr[...], w_ref[...]).astype(BF16)


def _in_proj(xf, g, w_main, l, w_gate, w_gate_t, bg_col, bg_row, tm=1024, tn=1024):
    n = xf.shape[0]
    tm = min(tm, n)
    grid = (n // tm, P_WIDTH // tn)
    return pl.pallas_call(
        _inproj_kernel,
        grid=grid,
        in_specs=[
            pl.BlockSpec((tm, D_MODEL), lambda i, j: (i, 0)),
            pl.BlockSpec((1, D_MODEL), lambda i, j: (0, 0)),
            pl.BlockSpec((None, D_MODEL, tn), lambda i, j: (l, 0, j)),
            pl.BlockSpec((D_MODEL, LANE), lambda i, j: (0, 0)),
            pl.BlockSpec((N_GATE, D_MODEL), lambda i, j: (0, 0)),
            pl.BlockSpec((1, LANE), lambda i, j: (0, 0)),
            pl.BlockSpec((N_GATE, 1), lambda i, j: (0, 0)),
        ],
        out_specs=[
            pl.BlockSpec((tm, tn), lambda i, j: (i, j)),
            pl.BlockSpec((tm, LANE), lambda i, j: (i, 0)),
            pl.BlockSpec((N_GATE, tm), lambda i, j: (0, i)),
        ],
        out_shape=[
            jax.ShapeDtypeStruct((n, P_WIDTH), BF16),
            jax.ShapeDtypeStruct((n, LANE), F32),
            jax.ShapeDtypeStruct((N_GATE, n), F32),
        ],
        scratch_shapes=[pltpu.VMEM((tm, D_MODEL), BF16)],
        compiler_params=_cparams(("parallel", "arbitrary")),
        name="in_proj",
    )(xf, g, w_main, w_gate, w_gate_t, bg_col, bg_row)


def _na_bias_table(rpb):
    qc = np.arange(GRID_W)[:, None]
    kc = np.arange(GRID_W)[None, :]
    cs = np.clip(qc - NA_KC // 2, 0, GRID_W - NA_KC)
    ok = (kc >= cs) & (kc < cs + NA_KC)
    dc = np.clip(kc - qc + NA_KC - 1, 0, 2 * NA_KC - 2)
    onehot = (dc[None] == np.arange(2 * NA_KC - 1)[:, None, None]).astype(np.float32)
    col = jnp.einsum("hrc,cqk->hrqk", rpb.astype(F32), onehot, precision=lax.Precision.HIGHEST)
    col = jnp.where(jnp.asarray(ok)[None, None], col, -1e30)
    tab = jnp.stack([col[:, d:d + NA_KR] for d in range(NA_KR)], axis=1)
    tab = tab.transpose(0, 1, 3, 2, 4)
    return tab.reshape(NA_HEADS, NA_KR, GRID_W, NA_KR * GRID_W)


def _na_kernel(q_ref, k_ref, v_ref, bias_ref, o_ref, *, rows, rps):
    scale = np.float32(NA_HEAD_DIM ** -0.5)
    win = NA_KR * GRID_W

    def body(i, carry):
        rr = [i * rps + u for u in range(rps)]
        rs = [jnp.clip(r - NA_KR // 2, 0, rows - NA_KR) for r in rr]
        q0 = [pl.multiple_of(r * GRID_W, GRID_W) for r in rr]
        k0 = [pl.multiple_of(x * GRID_W, GRID_W) for x in rs]
        s = [_dot_nt(q_ref[pl.ds(q0[u], GRID_W), :], k_ref[pl.ds(k0[u], win), :]) for u in range(rps)]
        p, inv = [], []
        for u in range(rps):
            su = s[u] * scale + bias_ref[rs[u] - rr[u] + NA_KR - 1]
            e = jnp.exp(su - jnp.max(su, axis=-1, keepdims=True))
            inv.append(1.0 / jnp.sum(e, axis=-1, keepdims=True))
            p.append(e.astype(BF16))
        for u in range(rps):
            o = _dot(p[u], v_ref[pl.ds(k0[u], win), :]) * inv[u]
            o_ref[pl.ds(q0[u], GRID_W), :] = o.astype(BF16)
        return carry

    lax.fori_loop(0, rows // rps, body, 0)


def _na(proj3, bias_tab):
    b, t, _ = proj3.shape
    rows = t // GRID_W
    blk = lambda off: pl.BlockSpec((None, t, NA_HEAD_DIM), lambda bi, h: (bi, 0, off + h))
    return pl.pallas_call(
        functools.partial(_na_kernel, rows=rows, rps=math.gcd(rows, NA_ROWS_PER_STEP)),
        grid=(b, NA_HEADS),
        in_specs=[
            blk(P_A // NA_HEAD_DIM),
            blk(P_A // NA_HEAD_DIM + NA_HEADS),
            blk(P_A // NA_HEAD_DIM + 2 * NA_HEADS),
            pl.BlockSpec((None, NA_KR, GRID_W, NA_KR * GRID_W), lambda bi, h: (h, 0, 0, 0)),
        ],
        out_specs=pl.BlockSpec((None, t, NA_HEAD_DIM), lambda bi, h: (bi, 0, h)),
        out_shape=jax.ShapeDtypeStruct((b, t, NA_WIDTH), BF16),
        compiler_params=_cparams(("parallel", "parallel")),
        name="na",
    )(proj3, proj3, proj3, bias_tab)


def _tri_product(tri_b, x, *, left):
    hi = x.astype(BF16)
    rest = x - hi.astype(F32)
    mid = rest.astype(BF16)
    lo = (rest - mid.astype(F32)).astype(BF16)
    if left:
        return (_dot(tri_b, lo) + _dot(tri_b, mid)) + _dot(tri_b, hi)
    return (_dot(lo, tri_b) + _dot(mid, tri_b)) + _dot(hi, tri_b)


def _mlstm_init(c_scr, n_scr, m_scr):
    @pl.when(pl.program_id(1) == 0)
    def _():
        c_scr[...] = jnp.zeros_like(c_scr)
        n_scr[...] = jnp.zeros_like(n_scr)
        m_scr[...] = jnp.zeros_like(m_scr)


def _mlstm_chunk(q_of, k_of, v_ref, gc_ref, gr_ref, c_scr, n_scr, m_scr, emit, *, reverse, L):
    ri = lax.broadcasted_iota(jnp.int32, (L, L), 0)
    ci = lax.broadcasted_iota(jnp.int32, (L, L), 1)
    mask = (ci >= ri) if reverse else (ci <= ri)
    mask_t = (ri >= ci) if reverse else (ri <= ci)
    gi = 2 if reverse else 0
    last = 0 if reverse else L - 1
    R = ML_STRIP

    gcol = gc_ref[...]
    grow = gr_ref[...]
    b_col_all = _tri_product(jnp.where(mask, 1.0, 0.0).astype(BF16), _log_sigmoid(gcol), left=True)
    b_row_all = _tri_product(jnp.where(mask_t, 1.0, 0.0).astype(BF16), _log_sigmoid(grow), left=False)
    b2_col_all = b_col_all * LOG2E

    heads = range(ML_HEADS)
    hsl = [slice(hd * ML_HEAD_DIM, (hd + 1) * ML_HEAD_DIM) for hd in heads]
    ii = [gi * ML_HEADS + hd for hd in heads]
    fi = [(gi + 1) * ML_HEADS + hd for hd in heads]
    m_prev = [m_scr[hd][:, 0:1] for hd in heads]
    c_prev = [c_scr[hd] for hd in heads]
    n_prev = [n_scr[hd] for hd in heads]
    q = [q_of(hd) for hd in heads]
    k = [k_of(hd) for hd in heads]
    vb = [v_ref[:, hsl[hd]] for hd in heads]

    strips = []
    for st in range(L // R):
        if reverse:
            c0, c1 = (st * R) // LANE * LANE, L
        else:
            c0, c1 = 0, -(-((st + 1) * R) // LANE) * LANE
        strips.append((st, slice(st * R, (st + 1) * R), slice(c0, c1)))
    units = [(hd, st, rows, cols) for hd in heads for (st, rows, cols) in strips]

    qk = [_dot_nt(q[hd][1][rows], k[hd][1][cols]) for (hd, st, rows, cols) in units]
    qc = [_dot(q[hd][1][rows], c_prev[hd].astype(BF16)) for (hd, st, rows, cols) in units]

    sb, scale_c, inv = [], [], []
    for u, (hd, st, rows, cols) in enumerate(units):
        c0, c1 = cols.start, cols.stop
        b2_col = b2_col_all[rows, fi[hd]:fi[hd] + 1]
        r2_row = (grow[ii[hd]:ii[hd] + 1, cols] - b_row_all[fi[hd]:fi[hd] + 1, cols]) * LOG2E
        sri = lax.broadcasted_iota(jnp.int32, (R, c1 - c0), 0) + st * R
        sci = lax.broadcasted_iota(jnp.int32, (R, c1 - c0), 1) + c0
        smask = (sci >= sri) if reverse else (sci <= sri)
        d2 = jnp.where(smask, b2_col + r2_row, -1e30)
        inter2 = b2_col + m_prev[hd] * LOG2E
        m2 = jnp.maximum(inter2, jnp.max(d2, axis=1, keepdims=True))
        s = qk[u] * jnp.exp2(d2 - m2)
        w_inter = jnp.exp2(inter2 - m2)
        nq = (jnp.sum(s, axis=1, keepdims=True)
              + w_inter * jnp.sum(q[hd][0][rows] * n_prev[hd], axis=1, keepdims=True))
        r = 1.0 / jnp.maximum(jnp.abs(nq), jnp.exp2(-m2))
        sb.append(s.astype(BF16))
        scale_c.append(w_inter * r)
        inv.append(r)

    for u, (hd, st, rows, cols) in enumerate(units):
        emit(hd, rows, _dot(sb[u], vb[hd][cols]) * inv[u] + qc[u] * scale_c[u])

    kws, decays, m_news = [], [], []
    for hd in heads:
        b_col = b_col_all[:, fi[hd]:fi[hd] + 1]
        b_last = b_col_all[last:last + 1, fi[hd]:fi[hd] + 1]
        wlog = b_last - b_col + gcol[:, ii[hd]:ii[hd] + 1]
        m_new = jnp.maximum(b_last + m_prev[hd], jnp.max(wlog, axis=0, keepdims=True))
        kws.append(k[hd][0] * jnp.exp(wlog - m_new))
        decays.append(jnp.exp(b_last + m_prev[hd] - m_new))
        m_news.append(m_new)
    eye = jnp.where(lax.broadcasted_iota(jnp.int32, (ML_HEAD_DIM, ML_HEAD_DIM), 0)
                    == lax.broadcasted_iota(jnp.int32, (ML_HEAD_DIM, ML_HEAD_DIM), 1), 1.0, 0.0).astype(BF16)
    kwt = [_dot_nt(eye, kws[hd].astype(BF16)).astype(BF16) for hd in heads]
    for hd in heads:
        c_scr[hd] = decays[hd] * c_prev[hd] + _dot(kwt[hd], vb[hd])
        n_scr[hd] = decays[hd] * n_prev[hd] + jnp.sum(kws[hd], axis=0, keepdims=True)
        m_scr[hd] = jnp.broadcast_to(m_news[hd], (1, LANE))


def _mlstm_fwd_kernel(qp_ref, q_ref, qn_ref, kp_ref, k_ref, kn_ref, v_ref, gc_ref, gr_ref, cw_ref,
                      h_ref, qs_ref, ks_ref, c_scr, n_scr, m_scr, *, nc, L):
    c = pl.program_id(1)
    _mlstm_init(c_scr, n_scr, m_scr)

    rowi = lax.broadcasted_iota(jnp.int32, (L, 1), 0)
    has_prev = (c > 0).astype(F32)
    has_next = (c < nc - 1).astype(F32)

    def conv_silu(p_ref, x_ref, n_ref, w):
        x = x_ref[...].astype(F32)
        prev_row = p_ref[HALO_ROWS - 1:HALO_ROWS, :].astype(F32) * has_prev
        next_row = n_ref[0:1, :].astype(F32) * has_next
        xp = jnp.where(rowi == 0, prev_row, pltpu.roll(x, 1, 0))
        xn = jnp.where(rowi == L - 1, next_row, pltpu.roll(x, L - 1, 0))
        y = xp * w[0:1] + x * w[1:2] + xn * w[2:3]
        return y * _sigmoid(y)

    cw = cw_ref[...]
    qs = conv_silu(qp_ref, q_ref, qn_ref, cw[:, :ML_WIDTH]) * np.float32(ML_HEAD_DIM ** -0.5)
    ks = conv_silu(kp_ref, k_ref, kn_ref, cw[:, ML_WIDTH:])
    qsb = qs.astype(BF16)
    ksb = ks.astype(BF16)
    qs_ref[...] = qsb
    ks_ref[...] = ksb

    def head(x, xb):
        return lambda hd: (x[:, hd * ML_HEAD_DIM:(hd + 1) * ML_HEAD_DIM],
                           xb[:, hd * ML_HEAD_DIM:(hd + 1) * ML_HEAD_DIM])

    def emit(hd, rows, h):
        h_ref[rows, hd * ML_HEAD_DIM:(hd + 1) * ML_HEAD_DIM] = h

    _mlstm_chunk(head(qs, qsb), head(ks, ksb), v_ref, gc_ref, gr_ref, c_scr, n_scr, m_scr, emit,
                 reverse=False, L=L)


def _mlstm_bwd_kernel(qs_ref, ks_ref, v_ref, gc_ref, gr_ref, hf_ref, o_ref, g_ref,
                      y_ref, c_scr, n_scr, m_scr, *, L):
    _mlstm_init(c_scr, n_scr, m_scr)

    def head(x_ref):
        def get(hd):
            xb = x_ref[:, hd * ML_HEAD_DIM:(hd + 1) * ML_HEAD_DIM]
            return xb.astype(F32), xb
        return get

    def emit(hd, rows, h):
        sl = slice(hd * ML_HEAD_DIM, (hd + 1) * ML_HEAD_DIM)
        y = _rms(h + hf_ref[rows, sl], g_ref[:, sl]) * _sigmoid(o_ref[rows, sl].astype(F32))
        y_ref[rows, sl] = y.astype(BF16)

    _mlstm_chunk(head(qs_ref), head(ks_ref), v_ref, gc_ref, gr_ref, c_scr, n_scr, m_scr, emit,
                 reverse=True, L=L)


def _mlstm_scratch():
    return [pltpu.VMEM((ML_HEADS, ML_HEAD_DIM, ML_HEAD_DIM), F32),
            pltpu.VMEM((ML_HEADS, 1, ML_HEAD_DIM), F32),
            pltpu.VMEM((ML_HEADS, 1, LANE), F32)]


def _mlstm_fwd(proj3, gate_col, gate_row, conv_w):
    b, t, _ = proj3.shape
    L = ML_CHUNK
    nc = t // L
    hb = L // HALO_ROWS
    n_halo = t // HALO_ROWS
    cur = lambda col: pl.BlockSpec((None, L, ML_WIDTH), lambda bi, c: (bi, c, col))
    prv = lambda col: pl.BlockSpec((None, HALO_ROWS, ML_WIDTH),
                                   lambda bi, c: (bi, jnp.maximum(c * hb - 1, 0), col))
    nxt = lambda col: pl.BlockSpec((None, HALO_ROWS, ML_WIDTH),
                                   lambda bi, c: (bi, jnp.minimum((c + 1) * hb, n_halo - 1), col))
    qc, kc, vc = P_B // ML_WIDTH, P_B // ML_WIDTH + 1, P_B // ML_WIDTH + 2
    return pl.pallas_call(
        functools.partial(_mlstm_fwd_kernel, nc=nc, L=L),
        grid=(b, nc),
        in_specs=[
            prv(qc), cur(qc), nxt(qc),
            prv(kc), cur(kc), nxt(kc),
            cur(vc),
            pl.BlockSpec((L, LANE), lambda bi, c: (bi * nc + c, 0)),
            pl.BlockSpec((N_GATE, L), lambda bi, c: (0, bi * nc + c)),
            pl.BlockSpec((3, 2 * ML_WIDTH), lambda bi, c: (0, 0)),
        ],
        out_specs=[cur(0), cur(0), cur(0)],
        out_shape=[jax.ShapeDtypeStruct((b, t, ML_WIDTH), F32),
                   jax.ShapeDtypeStruct((b, t, ML_WIDTH), BF16),
                   jax.ShapeDtypeStruct((b, t, ML_WIDTH), BF16)],
        scratch_shapes=_mlstm_scratch(),
        compiler_params=_cparams(("parallel", "arbitrary")),
        name="mlstm_fwd",
    )(proj3, proj3, proj3, proj3, proj3, proj3, proj3, gate_col, gate_row, conv_w)


def _mlstm_bwd(qs, ks, proj3, gate_col, gate_row, h_f, g):
    b, t, _ = proj3.shape
    L = ML_CHUNK
    nc = t // L
    cur = lambda col: pl.BlockSpec((None, L, ML_WIDTH), lambda bi, c: (bi, nc - 1 - c, col))
    vc, oc = P_B // ML_WIDTH + 2, P_B // ML_WIDTH + 3
    return pl.pallas_call(
        functools.partial(_mlstm_bwd_kernel, L=L),
        grid=(b, nc),
        in_specs=[
            cur(0), cur(0), cur(vc),
            pl.BlockSpec((L, LANE), lambda bi, c: (bi * nc + nc - 1 - c, 0)),
            pl.BlockSpec((N_GATE, L), lambda bi, c: (0, bi * nc + nc - 1 - c)),
            cur(0), cur(oc),
            pl.BlockSpec((1, ML_WIDTH), lambda bi, c: (0, 0)),
        ],
        out_specs=cur(0),
        out_shape=jax.ShapeDtypeStruct((b, t, ML_WIDTH), BF16),
        scratch_shapes=_mlstm_scratch(),
        compiler_params=_cparams(("parallel", "arbitrary")),
        name="mlstm_bwd",
    )(qs, ks, proj3, gate_col, gate_row, h_f, proj3, g)


def _sg_kernel(u_ref, v_ref, g_ref, ws_ref, bst_ref, y_ref, *, chunks):
    for cidx in range(chunks):
        rs = slice(cidx * SG_CHUNK, (cidx + 1) * SG_CHUNK)
        zu = _gelu_tanh(u_ref[rs, :].astype(F32))
        zv = _gelu_tanh(v_ref[rs, :].astype(F32))
        vb = _rms(zv, g_ref[...]).astype(BF16)
        for gidx in range(SG_GROUPS):
            cs = slice(gidx * SG_GROUP_DIM, (gidx + 1) * SG_GROUP_DIM)
            mixed = _dot(ws_ref[gidx], vb[:, cs]) + bst_ref[:, gidx:gidx + 1]
            y_ref[rs, cs] = (zu[:, cs] * mixed).astype(BF16)


def _sg(proj, g, ws, bst, chunks=4):
    n = proj.shape[0]
    tm = chunks * SG_CHUNK
    row = lambda col: pl.BlockSpec((tm, SG_WIDTH), lambda i: (i, col))
    return pl.pallas_call(
        functools.partial(_sg_kernel, chunks=chunks),
        grid=(n // tm,),
        in_specs=[row(P_C // SG_WIDTH), row(P_C // SG_WIDTH + 1),
                  pl.BlockSpec((1, SG_WIDTH), lambda i: (0, 0)),
                  pl.BlockSpec((SG_GROUPS, SG_CHUNK, SG_CHUNK), lambda i: (0, 0, 0)),
                  pl.BlockSpec((SG_CHUNK, SG_GROUPS), lambda i: (0, 0))],
        out_specs=row(0),
        out_shape=jax.ShapeDtypeStruct((n, SG_WIDTH), BF16),
        compiler_params=_cparams(("parallel",)),
        name="sg",
    )(proj, proj, g, ws, bst)


def _mix_kernel(ya_ref, yb_ref, yc_ref, ga_ref, gb_ref, gc_ref, wa_ref, wb_ref, wc_ref,
                wo_ref, x_ref, g_ref, o_ref, m_scr, *, nj, tn):
    j = pl.program_id(1)

    def branch(y_ref, w_ref, gate_ref):
        return _sigmoid(gate_ref[...].astype(F32)) * _dot(y_ref[...], w_ref[j])

    merged = branch(ya_ref, wa_ref, ga_ref) + branch(yb_ref, wb_ref, gb_ref) + branch(yc_ref, wc_ref, gc_ref)
    m_scr[j] = merged.astype(BF16)

    @pl.when(j == nj - 1)
    def _():
        acc = _dot(m_scr[0], wo_ref[0:tn, :])
        for jj in range(1, nj):
            acc = acc + _dot(m_scr[jj], wo_ref[jj * tn:(jj + 1) * tn, :])
        o_ref[...] = x_ref[...] + _rms(acc, g_ref[...])


def _mix_out(ya, yb, yc, proj, wa, wb, wc, wo, l, xf, g, tm=512):
    n = xf.shape[0]
    _, nj, _, tn = wa.shape
    yspec = pl.BlockSpec((tm, NA_WIDTH), lambda i, j: (i, 0))
    gate = lambda k: pl.BlockSpec((tm, tn), lambda i, j: (i, (P_G + k * D_MODEL) // tn + j))
    wspec = pl.BlockSpec((None, nj, NA_WIDTH, tn), lambda i, j: (l, 0, 0, 0), pipeline_mode=pl.Buffered(1))
    return pl.pallas_call(
        functools.partial(_mix_kernel, nj=nj, tn=tn),
        grid=(n // tm, nj),
        in_specs=[yspec, yspec, yspec, gate(0), gate(1), gate(2), wspec, wspec, wspec,
                  pl.BlockSpec((None, D_MODEL, D_MODEL), lambda i, j: (l, 0, 0), pipeline_mode=pl.Buffered(1)),
                  pl.BlockSpec((tm, D_MODEL), lambda i, j: (i, 0)),
                  pl.BlockSpec((1, D_MODEL), lambda i, j: (0, 0))],
        out_specs=pl.BlockSpec((tm, D_MODEL), lambda i, j: (i, 0)),
        out_shape=jax.ShapeDtypeStruct((n, D_MODEL), F32),
        scratch_shapes=[pltpu.VMEM((nj, tm, tn), BF16)],
        compiler_params=_cparams(("parallel", "arbitrary")),
        name="mix_out",
    )(ya, yb, yc, proj, proj, proj, wa, wb, wc, wo, xf, g)


def _mlp_kernel(x_ref, g1_ref, wu_ref, wd_ref, g2_ref, o_ref, h_scr, acc_scr, *, nj):
    j = pl.program_id(1)

    @pl.when(j == 0)
    def _():
        h_scr[...] = _rms(x_ref[...], g1_ref[...]).astype(BF16)
        acc_scr[...] = jnp.zeros_like(acc_scr)

    a = jnp.maximum(_dot(h_scr[...], wu_ref[...]), 0.0)
    acc_scr[...] += _dot((a * a).astype(BF16), wd_ref[...])

    @pl.when(j == nj - 1)
    def _():
        o_ref[...] = x_ref[...] + _rms(acc_scr[...], g2_ref[...])


def _mlp(xf, g1, wu, wd, l, g2, tm=512, tf=1024):
    n = xf.shape[0]
    nj = D_FF // tf
    return pl.pallas_call(
        functools.partial(_mlp_kernel, nj=nj),
        grid=(n // tm, nj),
        in_specs=[pl.BlockSpec((tm, D_MODEL), lambda i, j: (i, 0)),
                  pl.BlockSpec((1, D_MODEL), lambda i, j: (0, 0)),
                  pl.BlockSpec((None, D_MODEL, tf), lambda i, j: (l, 0, j)),
                  pl.BlockSpec((None, tf, D_MODEL), lambda i, j: (l, j, 0)),
                  pl.BlockSpec((1, D_MODEL), lambda i, j: (0, 0))],
        out_specs=pl.BlockSpec((tm, D_MODEL), lambda i, j: (i, 0)),
        out_shape=jax.ShapeDtypeStruct((n, D_MODEL), F32),
        scratch_shapes=[pltpu.VMEM((tm, D_MODEL), BF16), pltpu.VMEM((tm, D_MODEL), F32)],
        compiler_params=_cparams(("parallel", "arbitrary")),
        name="mlp",
    )(xf, g1, wu, wd, g2)


def _regroup_kernel(xt_ref, xg_ref, o_ref, gt_ref):
    o_ref[...] = xt_ref[0].T.astype(BF16)

    @pl.when(pl.program_id(1) == 0)
    def _():
        gt_ref[...] = xg_ref[0].astype(BF16)


def _regroup_cast(w_in_t, tc=512):
    depth = w_in_t.shape[0]

    def src_row(l, c):
        r = c * tc
        return (l, pl.multiple_of(jnp.where(r < OFF_BG, r, r + N_GATE), N_GATE), 0)

    return pl.pallas_call(
        _regroup_kernel,
        grid=(depth, P_WIDTH // tc),
        in_specs=[pl.BlockSpec((pl.Element(1), pl.Element(tc), pl.Element(D_MODEL)), src_row),
                  pl.BlockSpec((pl.Element(1), pl.Element(N_GATE), pl.Element(D_MODEL)),
                               lambda l, c: (l, OFF_BG, 0))],
        out_specs=[pl.BlockSpec((None, D_MODEL, tc), lambda l, c: (l, 0, c)),
                   pl.BlockSpec((None, N_GATE, D_MODEL), lambda l, c: (l, 0, 0))],
        out_shape=[jax.ShapeDtypeStruct((depth, D_MODEL, P_WIDTH), BF16),
                   jax.ShapeDtypeStruct((depth, N_GATE, D_MODEL), BF16)],
        compiler_params=_cparams(("parallel", "arbitrary")),
        name="regroup_cast",
    )(w_in_t, w_in_t)


def _col_tiles(w, tn=MIX_TN):
    d, k, n = w.shape
    return w.astype(BF16).reshape(d, k, n // tn, tn).transpose(0, 2, 1, 3)


def _prep_shared(w_in, w_a, w_b, w_c, w_out, w_up, w_down):
    w_main, w_gate_t = _regroup_cast(jnp.swapaxes(w_in, 1, 2))
    return dict(w_main=w_main, w_gate_t=w_gate_t,
                w_a=_col_tiles(w_a), w_b=_col_tiles(w_b), w_c=_col_tiles(w_c),
                w_out=w_out.astype(BF16), w_up=w_up.astype(BF16), w_down=w_down.astype(BF16))


def _prep_layer(l, shared, pre_mix_g, post_mix_g, pre_mlp_g, post_mlp_g, b_gate, conv_w, na_rpb,
                ml_norm_g, sg_norm_g, w_s, b_s):
    w_gt = shared["w_gate_t"][l]
    p = dict(shared)
    p.update(
        l=l,
        pre_mix_g=pre_mix_g[l].reshape(1, D_MODEL), post_mix_g=post_mix_g[l].reshape(1, D_MODEL),
        pre_mlp_g=pre_mlp_g[l].reshape(1, D_MODEL), post_mlp_g=post_mlp_g[l].reshape(1, D_MODEL),
        w_gate=jnp.pad(w_gt.T, ((0, 0), (0, LANE - N_GATE))), w_gate_t=w_gt,
        bg_col=jnp.pad(b_gate[l].astype(F32), (0, LANE - N_GATE)).reshape(1, LANE),
        bg_row=b_gate[l].astype(F32).reshape(N_GATE, 1),
        conv_w=conv_w[l].astype(F32), bias_tab=_na_bias_table(na_rpb[l]),
        ml_norm_g=ml_norm_g[l].reshape(1, ML_WIDTH), sg_norm_g=sg_norm_g[l].reshape(1, SG_WIDTH),
        w_s=w_s[l].astype(BF16), bst=b_s[l].astype(F32).T,
    )
    return p


def _trunk_layer(x, p):
    b, t, _ = x.shape
    n = b * t
    l = p["l"]
    xf = x.reshape(n, D_MODEL)
    proj, gate_col, gate_row = _in_proj(xf, p["pre_mix_g"], p["w_main"], l, p["w_gate"], p["w_gate_t"],
                                        p["bg_col"], p["bg_row"])
    proj3 = proj.reshape(b, t, P_WIDTH)
    y_a = _na(proj3, p["bias_tab"]).reshape(n, NA_WIDTH)
    h_f, qs, ks = _mlstm_fwd(proj3, gate_col, gate_row, p["conv_w"])
    y_b = _mlstm_bwd(qs, ks, proj3, gate_col, gate_row, h_f, p["ml_norm_g"]).reshape(n, ML_WIDTH)
    y_c = _sg(proj, p["sg_norm_g"], p["w_s"], p["bst"])
    x1 = _mix_out(y_a, y_b, y_c, proj, p["w_a"], p["w_b"], p["w_c"], p["w_out"], l, xf, p["post_mix_g"])
    x2 = _mlp(x1, p["pre_mlp_g"], p["w_up"], p["w_down"], l, p["post_mlp_g"])
    return x2.reshape(b, t, D_MODEL)


def kernel(x_prompt, x_sample, pre_mix_g, post_mix_g, pre_mlp_g, post_mlp_g, w_in, b_gate, conv_w,
           na_rpb, ml_norm_g, sg_norm_g, w_s, b_s, w_a, w_b, w_c, w_out, w_up, w_down):
    shared = _prep_shared(w_in, w_a, w_b, w_c, w_out, w_up, w_down)
    layers = [_prep_layer(l, shared, pre_mix_g, post_mix_g, pre_mlp_g, post_mlp_g, b_gate, conv_w,
                          na_rpb, ml_norm_g, sg_norm_g, w_s, b_s)
              for l in range(w_in.shape[0])]

    def run(x):
        for p in layers:
            x = _trunk_layer(x, p)
        return x

    return (run(x_prompt), run(x_sample))
```

```python
import functools
import math

import numpy as np
import jax
import jax.numpy as jnp
from jax import lax
from jax.experimental import pallas as pl
from jax.experimental.pallas import tpu as pltpu

F32 = jnp.float32
BF16 = jnp.bfloat16

D_MODEL = 2048
GRID_W = 64
EPS = 1e-6

NA_HEADS = 8
NA_HEAD_DIM = 128
NA_WIDTH = NA_HEADS * NA_HEAD_DIM
NA_KR = 8
NA_KC = 16
NA_ROWS_PER_STEP = 16

ML_HEADS = 4
ML_HEAD_DIM = 256
ML_WIDTH = ML_HEADS * ML_HEAD_DIM
ML_CHUNK = 256
ML_STRIP = 256
MIX_TN = 512

SG_GROUPS = 8
SG_CHUNK = 128
SG_WIDTH = 1024
SG_GROUP_DIM = SG_WIDTH // SG_GROUPS

D_FF = 4 * D_MODEL
N_BRANCH = 3

OFF_A = 0
OFF_B = OFF_A + 3 * NA_WIDTH
OFF_BG = OFF_B + 4 * ML_WIDTH
OFF_C = OFF_BG + 4 * ML_HEADS
OFF_G = OFF_C + 2 * SG_WIDTH
D_IN = OFF_G + N_BRANCH * D_MODEL
N_GATE = 4 * ML_HEADS

P_A = 0
P_B = P_A + 3 * NA_WIDTH
P_C = P_B + 4 * ML_WIDTH
P_G = P_C + 2 * SG_WIDTH
P_WIDTH = P_G + N_BRANCH * D_MODEL

LANE = 128
HALO_ROWS = 16
VMEM_LIMIT = 56 * 1024 * 1024
LOG2E = np.float32(1.4426950408889634)


def _cparams(sem, vmem=VMEM_LIMIT, **kw):
    return pltpu.CompilerParams(dimension_semantics=sem, vmem_limit_bytes=vmem, **kw)


def _rms(x, g):
    return x * lax.rsqrt(jnp.mean(x * x, axis=-1, keepdims=True) + EPS) * g


def _sigmoid(x):
    return 1.0 / (1.0 + jnp.exp(-x))


def _log_sigmoid(x):
    return jnp.minimum(x, 0.0) - jnp.log(1.0 + jnp.exp(-jnp.abs(x)))


def _gelu_tanh(x):
    k = -2.0 * np.sqrt(2.0 / np.pi) * float(LOG2E)
    z = x * (np.float32(k * 0.044715) * (x * x) + np.float32(k))
    return x * (1.0 / (1.0 + jnp.exp2(z)))


def _dot(a, b):
    return jnp.dot(a, b, preferred_element_type=F32)


def _dot_nt(a, b):
    return lax.dot_general(a, b, (((1,), (1,)), ((), ())), preferred_element_type=F32)


def _dot_tn(a, b):
    return lax.dot_general(a, b, (((0,), (0,)), ((), ())), preferred_element_type=F32)


def _inproj_kernel(x_ref, g_ref, w_ref, wg_ref, wgt_ref, bgc_ref, bgr_ref,
                   o_ref, gc_ref, gr_ref, h_scr):
    @pl.when(pl.program_id(1) == 0)
    def _():
        hb = _rms(x_ref[...], g_ref[...]).astype(BF16)
        h_scr[...] = hb
        gc_ref[...] = _dot(hb, wg_ref[...]) + bgc_ref[...]
        gr_ref[...] = _dot_nt(wgt_ref[...], hb) + bgr_ref[...]

    o_ref[...] = _dot(h_scr[...], w_ref[...]).astype(BF16)


def _in_proj(xf, g, w_main, l, w_gate, w_gate_t, bg_col, bg_row, tm=1024, tn=1536):
    n = xf.shape[0]
    tm = min(tm, n)
    grid = (n // tm, P_WIDTH // tn)
    return pl.pallas_call(
        _inproj_kernel,
        grid=grid,
        in_specs=[
            pl.BlockSpec((tm, D_MODEL), lambda i, j: (i, 0)),
            pl.BlockSpec((1, D_MODEL), lambda i, j: (0, 0)),
            pl.BlockSpec((None, D_MODEL, tn), lambda i, j: (l, 0, j)),
            pl.BlockSpec((D_MODEL, LANE), lambda i, j: (0, 0)),
            pl.BlockSpec((N_GATE, D_MODEL), lambda i, j: (0, 0)),
            pl.BlockSpec((1, LANE), lambda i, j: (0, 0)),
            pl.BlockSpec((N_GATE, 1), lambda i, j: (0, 0)),
        ],
        out_specs=[
            pl.BlockSpec((tm, tn), lambda i, j: (i, j)),
            pl.BlockSpec((tm, LANE), lambda i, j: (i, 0)),
            pl.BlockSpec((N_GATE, tm), lambda i, j: (0, i)),
        ],
        out_shape=[
            jax.ShapeDtypeStruct((n, P_WIDTH), BF16),
            jax.ShapeDtypeStruct((n, LANE), F32),
            jax.ShapeDtypeStruct((N_GATE, n), F32),
        ],
        scratch_shapes=[pltpu.VMEM((tm, D_MODEL), BF16)],
        compiler_params=_cparams(("parallel", "arbitrary")),
        name="in_proj",
    )(xf, g, w_main, w_gate, w_gate_t, bg_col, bg_row)


def _na_bias_table(rpb):
    qc = np.arange(GRID_W)[:, None]
    kc = np.arange(GRID_W)[None, :]
    cs = np.clip(qc - NA_KC // 2, 0, GRID_W - NA_KC)
    ok = (kc >= cs) & (kc < cs + NA_KC)
    dc = np.clip(kc - qc + NA_KC - 1, 0, 2 * NA_KC - 2)
    onehot = (dc[None] == np.arange(2 * NA_KC - 1)[:, None, None]).astype(np.float32)
    col = jnp.einsum("hrc,cqk->hrqk", rpb.astype(F32), onehot, precision=lax.Precision.HIGHEST)
    col = jnp.where(jnp.asarray(ok)[None, None], col * LOG2E, -1e30)
    tab = jnp.stack([col[:, d:d + NA_KR] for d in range(NA_KR)], axis=1)
    tab = tab.transpose(0, 1, 3, 2, 4)
    return tab.reshape(NA_HEADS, NA_KR, GRID_W, NA_KR * GRID_W)


def _na_kernel(q_ref, k_ref, v_ref, bias_ref, o_ref, *, rows, rps):
    scale = np.float32(NA_HEAD_DIM ** -0.5 * float(LOG2E))
    win = NA_KR * GRID_W

    def body(i, carry):
        rr = [i * rps + u for u in range(rps)]
        rs = [jnp.clip(r - NA_KR // 2, 0, rows - NA_KR) for r in rr]
        q0 = [pl.multiple_of(r * GRID_W, GRID_W) for r in rr]
        k0 = [pl.multiple_of(x * GRID_W, GRID_W) for x in rs]
        s = [_dot_nt(q_ref[pl.ds(q0[u], GRID_W), :], k_ref[pl.ds(k0[u], win), :]) for u in range(rps)]
        p, inv = [], []
        for u in range(rps):
            su = s[u] * scale + bias_ref[rs[u] - rr[u] + NA_KR - 1]
            e = jnp.exp2(su - jnp.max(su, axis=-1, keepdims=True))
            inv.append(1.0 / jnp.sum(e, axis=-1, keepdims=True))
            p.append(e.astype(BF16))
        for u in range(rps):
            o = _dot(p[u], v_ref[pl.ds(k0[u], win), :]) * inv[u]
            o_ref[pl.ds(q0[u], GRID_W), :] = o.astype(BF16)
        return carry

    lax.fori_loop(0, rows // rps, body, 0)


def _na(proj3, bias_tab):
    b, t, _ = proj3.shape
    rows = t // GRID_W
    blk = lambda off: pl.BlockSpec((None, t, NA_HEAD_DIM), lambda bi, h: (bi, 0, off + h))
    return pl.pallas_call(
        functools.partial(_na_kernel, rows=rows, rps=math.gcd(rows, NA_ROWS_PER_STEP)),
        grid=(b, NA_HEADS),
        in_specs=[
            blk(P_A // NA_HEAD_DIM),
            blk(P_A // NA_HEAD_DIM + NA_HEADS),
            blk(P_A // NA_HEAD_DIM + 2 * NA_HEADS),
            pl.BlockSpec((None, NA_KR, GRID_W, NA_KR * GRID_W), lambda bi, h: (h, 0, 0, 0)),
        ],
        out_specs=pl.BlockSpec((None, t, NA_HEAD_DIM), lambda bi, h: (bi, 0, h)),
        out_shape=jax.ShapeDtypeStruct((b, t, NA_WIDTH), BF16),
        compiler_params=_cparams(("parallel", "parallel")),
        name="na",
    )(proj3, proj3, proj3, bias_tab)


def _tri_product(tri_b, x, *, left):
    hi = x.astype(BF16)
    rest = x - hi.astype(F32)
    mid = rest.astype(BF16)
    lo = (rest - mid.astype(F32)).astype(BF16)
    if left:
        return (_dot(tri_b, lo) + _dot(tri_b, mid)) + _dot(tri_b, hi)
    return (_dot(lo, tri_b) + _dot(mid, tri_b)) + _dot(hi, tri_b)


def _mlstm_init(c_scr, n_scr, m_scr):
    @pl.when(pl.program_id(1) == 0)
    def _():
        c_scr[...] = jnp.zeros_like(c_scr)
        n_scr[...] = jnp.zeros_like(n_scr)
        m_scr[...] = jnp.zeros_like(m_scr)


def _mlstm_chunk(q_of, k_of, v_ref, gc_ref, gr_ref, c_scr, n_scr, m_scr, emit, *, reverse, L):
    ri = lax.broadcasted_iota(jnp.int32, (L, L), 0)
    ci = lax.broadcasted_iota(jnp.int32, (L, L), 1)
    mask = (ci >= ri) if reverse else (ci <= ri)
    mask_t = (ri >= ci) if reverse else (ri <= ci)
    gi = 2 if reverse else 0
    last = 0 if reverse else L - 1
    R = ML_STRIP

    gcol = gc_ref[...]
    grow = gr_ref[...]
    b_col_all = _tri_product(jnp.where(mask, 1.0, 0.0).astype(BF16), _log_sigmoid(gcol), left=True)
    b_row_all = _tri_product(jnp.where(mask_t, 1.0, 0.0).astype(BF16), _log_sigmoid(grow), left=False)
    b2_col_all = b_col_all * LOG2E

    heads = range(ML_HEADS)
    hsl = [slice(hd * ML_HEAD_DIM, (hd + 1) * ML_HEAD_DIM) for hd in heads]
    ii = [gi * ML_HEADS + hd for hd in heads]
    fi = [(gi + 1) * ML_HEADS + hd for hd in heads]
    m_prev = [m_scr[hd][:, 0:1] for hd in heads]
    c_prev = [c_scr[hd] for hd in heads]
    n_prev = [n_scr[hd] for hd in heads]
    q = [q_of(hd) for hd in heads]
    k = [k_of(hd) for hd in heads]
    vb = [v_ref[:, hsl[hd]] for hd in heads]

    strips = []
    for st in range(L // R):
        if reverse:
            c0, c1 = (st * R) // LANE * LANE, L
        else:
            c0, c1 = 0, -(-((st + 1) * R) // LANE) * LANE
        strips.append((st, slice(st * R, (st + 1) * R), slice(c0, c1)))
    units = [(hd, st, rows, cols) for hd in heads for (st, rows, cols) in strips]

    qk = [_dot_nt(q[hd][1][rows], k[hd][1][cols]) for (hd, st, rows, cols) in units]
    qc = [_dot(q[hd][1][rows], c_prev[hd].astype(BF16)) for (hd, st, rows, cols) in units]

    sb, scale_c, inv = [], [], []
    for u, (hd, st, rows, cols) in enumerate(units):
        c0, c1 = cols.start, cols.stop
        b2_col = b2_col_all[rows, fi[hd]:fi[hd] + 1]
        r2_row = (grow[ii[hd]:ii[hd] + 1, cols] - b_row_all[fi[hd]:fi[hd] + 1, cols]) * LOG2E
        sri = lax.broadcasted_iota(jnp.int32, (R, c1 - c0), 0) + st * R
        sci = lax.broadcasted_iota(jnp.int32, (R, c1 - c0), 1) + c0
        smask = (sci >= sri) if reverse else (sci <= sri)
        d2 = jnp.where(smask, b2_col + r2_row, -1e30)
        inter2 = b2_col + m_prev[hd] * LOG2E
        m2 = jnp.maximum(inter2, jnp.max(d2, axis=1, keepdims=True))
        s = qk[u] * jnp.exp2(d2 - m2)
        w_inter = jnp.exp2(inter2 - m2)
        nq = (jnp.sum(s, axis=1, keepdims=True)
              + w_inter * jnp.sum(q[hd][0][rows] * n_prev[hd], axis=1, keepdims=True))
        r = 1.0 / jnp.maximum(jnp.abs(nq), jnp.exp2(-m2))
        sb.append(s.astype(BF16))
        scale_c.append(w_inter * r)
        inv.append(r)

    for u, (hd, st, rows, cols) in enumerate(units):
        emit(hd, rows, _dot(sb[u], vb[hd][cols]) * inv[u] + qc[u] * scale_c[u])

    kws, decays, m_news = [], [], []
    for hd in heads:
        b_col = b_col_all[:, fi[hd]:fi[hd] + 1]
        b_last = b_col_all[last:last + 1, fi[hd]:fi[hd] + 1]
        wlog = b_last - b_col + gcol[:, ii[hd]:ii[hd] + 1]
        m_new = jnp.maximum(b_last + m_prev[hd], jnp.max(wlog, axis=0, keepdims=True))
        kws.append(k[hd][0] * jnp.exp(wlog - m_new))
        decays.append(jnp.exp(b_last + m_prev[hd] - m_new))
        m_news.append(m_new)
    eye = jnp.where(lax.broadcasted_iota(jnp.int32, (ML_HEAD_DIM, ML_HEAD_DIM), 0)
                    == lax.broadcasted_iota(jnp.int32, (ML_HEAD_DIM, ML_HEAD_DIM), 1), 1.0, 0.0).astype(BF16)
    kwt = [_dot_nt(eye, kws[hd].astype(BF16)).astype(BF16) for hd in heads]
    for hd in heads:
        c_scr[hd] = decays[hd] * c_prev[hd] + _dot(kwt[hd], vb[hd])
        n_scr[hd] = decays[hd] * n_prev[hd] + jnp.sum(kws[hd], axis=0, keepdims=True)
        m_scr[hd] = jnp.broadcast_to(m_news[hd], (1, LANE))


def _mlstm_fwd_kernel(qp_ref, q_ref, qn_ref, kp_ref, k_ref, kn_ref, v_ref, gc_ref, gr_ref, cw_ref,
                      h_ref, qs_ref, ks_ref, c_scr, n_scr, m_scr, *, nc, L):
    c = pl.program_id(1)
    _mlstm_init(c_scr, n_scr, m_scr)

    rowi = lax.broadcasted_iota(jnp.int32, (L, 1), 0)
    has_prev = (c > 0).astype(F32)
    has_next = (c < nc - 1).astype(F32)

    def conv_silu(p_ref, x_ref, n_ref, w):
        x = x_ref[...].astype(F32)
        prev_row = p_ref[HALO_ROWS - 1:HALO_ROWS, :].astype(F32) * has_prev
        next_row = n_ref[0:1, :].astype(F32) * has_next
        xp = jnp.where(rowi == 0, prev_row, pltpu.roll(x, 1, 0))
        xn = jnp.where(rowi == L - 1, next_row, pltpu.roll(x, L - 1, 0))
        y = xp * w[0:1] + x * w[1:2] + xn * w[2:3]
        return y * _sigmoid(y)

    cw = cw_ref[...]
    qs = conv_silu(qp_ref, q_ref, qn_ref, cw[:, :ML_WIDTH]) * np.float32(ML_HEAD_DIM ** -0.5)
    ks = conv_silu(kp_ref, k_ref, kn_ref, cw[:, ML_WIDTH:])
    qsb = qs.astype(BF16)
    ksb = ks.astype(BF16)
    qs_ref[...] = qsb
    ks_ref[...] = ksb

    def head(x, xb):
        return lambda hd: (x[:, hd * ML_HEAD_DIM:(hd + 1) * ML_HEAD_DIM],
                           xb[:, hd * ML_HEAD_DIM:(hd + 1) * ML_HEAD_DIM])

    def emit(hd, rows, h):
        h_ref[rows, hd * ML_HEAD_DIM:(hd + 1) * ML_HEAD_DIM] = h

    _mlstm_chunk(head(qs, qsb), head(ks, ksb), v_ref, gc_ref, gr_ref, c_scr, n_scr, m_scr, emit,
                 reverse=False, L=L)


def _mlstm_bwd_kernel(qs_ref, ks_ref, v_ref, gc_ref, gr_ref, hf_ref, o_ref, g_ref,
                      y_ref, c_scr, n_scr, m_scr, *, L):
    _mlstm_init(c_scr, n_scr, m_scr)

    def head(x_ref):
        def get(hd):
            xb = x_ref[:, hd * ML_HEAD_DIM:(hd + 1) * ML_HEAD_DIM]
            return xb.astype(F32), xb
        return get

    def emit(hd, rows, h):
        sl = slice(hd * ML_HEAD_DIM, (hd + 1) * ML_HEAD_DIM)
        y = _rms(h + hf_ref[rows, sl], g_ref[:, sl]) * _sigmoid(o_ref[rows, sl].astype(F32))
        y_ref[rows, sl] = y.astype(BF16)

    _mlstm_chunk(head(qs_ref), head(ks_ref), v_ref, gc_ref, gr_ref, c_scr, n_scr, m_scr, emit,
                 reverse=True, L=L)


def _mlstm_scratch():
    return [pltpu.VMEM((ML_HEADS, ML_HEAD_DIM, ML_HEAD_DIM), F32),
            pltpu.VMEM((ML_HEADS, 1, ML_HEAD_DIM), F32),
            pltpu.VMEM((ML_HEADS, 1, LANE), F32)]


def _mlstm_fwd(proj3, gate_col, gate_row, conv_w):
    b, t, _ = proj3.shape
    L = ML_CHUNK
    nc = t // L
    hb = L // HALO_ROWS
    n_halo = t // HALO_ROWS
    cur = lambda col: pl.BlockSpec((None, L, ML_WIDTH), lambda bi, c: (bi, c, col))
    prv = lambda col: pl.BlockSpec((None, HALO_ROWS, ML_WIDTH),
                                   lambda bi, c: (bi, jnp.maximum(c * hb - 1, 0), col))
    nxt = lambda col: pl.BlockSpec((None, HALO_ROWS, ML_WIDTH),
                                   lambda bi, c: (bi, jnp.minimum((c + 1) * hb, n_halo - 1), col))
    qc, kc, vc = P_B // ML_WIDTH, P_B // ML_WIDTH + 1, P_B // ML_WIDTH + 2
    return pl.pallas_call(
        functools.partial(_mlstm_fwd_kernel, nc=nc, L=L),
        grid=(b, nc),
        in_specs=[
            prv(qc), cur(qc), nxt(qc),
            prv(kc), cur(kc), nxt(kc),
            cur(vc),
            pl.BlockSpec((L, LANE), lambda bi, c: (bi * nc + c, 0)),
            pl.BlockSpec((N_GATE, L), lambda bi, c: (0, bi * nc + c)),
            pl.BlockSpec((3, 2 * ML_WIDTH), lambda bi, c: (0, 0)),
        ],
        out_specs=[cur(0), cur(0), cur(0)],
        out_shape=[jax.ShapeDtypeStruct((b, t, ML_WIDTH), F32),
                   jax.ShapeDtypeStruct((b, t, ML_WIDTH), BF16),
                   jax.ShapeDtypeStruct((b, t, ML_WIDTH), BF16)],
        scratch_shapes=_mlstm_scratch(),
        compiler_params=_cparams(("parallel", "arbitrary")),
        name="mlstm_fwd",
    )(proj3, proj3, proj3, proj3, proj3, proj3, proj3, gate_col, gate_row, conv_w)


def _mlstm_bwd(qs, ks, proj3, gate_col, gate_row, h_f, g):
    b, t, _ = proj3.shape
    L = ML_CHUNK
    nc = t // L
    cur = lambda col: pl.BlockSpec((None, L, ML_WIDTH), lambda bi, c: (bi, nc - 1 - c, col))
    vc, oc = P_B // ML_WIDTH + 2, P_B // ML_WIDTH + 3
    return pl.pallas_call(
        functools.partial(_mlstm_bwd_kernel, L=L),
        grid=(b, nc),
        in_specs=[
            cur(0), cur(0), cur(vc),
            pl.BlockSpec((L, LANE), lambda bi, c: (bi * nc + nc - 1 - c, 0)),
            pl.BlockSpec((N_GATE, L), lambda bi, c: (0, bi * nc + nc - 1 - c)),
            cur(0), cur(oc),
            pl.BlockSpec((1, ML_WIDTH), lambda bi, c: (0, 0)),
        ],
        out_specs=cur(0),
        out_shape=jax.ShapeDtypeStruct((b, t, ML_WIDTH), BF16),
        scratch_shapes=_mlstm_scratch(),
        compiler_params=_cparams(("parallel", "arbitrary")),
        name="mlstm_bwd",
    )(qs, ks, proj3, gate_col, gate_row, h_f, proj3, g)


def _sg_kernel(u_ref, v_ref, g_ref, ws_ref, bst_ref, y_ref, *, chunks):
    for cidx in range(chunks):
        rs = slice(cidx * SG_CHUNK, (cidx + 1) * SG_CHUNK)
        zu = _gelu_tanh(u_ref[rs, :].astype(F32))
        zv = _gelu_tanh(v_ref[rs, :].astype(F32))
        vb = _rms(zv, g_ref[...]).astype(BF16)
        for gidx in range(SG_GROUPS):
            cs = slice(gidx * SG_GROUP_DIM, (gidx + 1) * SG_GROUP_DIM)
            mixed = _dot(ws_ref[gidx], vb[:, cs]) + bst_ref[:, gidx:gidx + 1]
            y_ref[rs, cs] = (zu[:, cs] * mixed).astype(BF16)


def _sg(proj, g, ws, bst, chunks=4):
    n = proj.shape[0]
    tm = chunks * SG_CHUNK
    row = lambda col: pl.BlockSpec((tm, SG_WIDTH), lambda i: (i, col))
    return pl.pallas_call(
        functools.partial(_sg_kernel, chunks=chunks),
        grid=(n // tm,),
        in_specs=[row(P_C // SG_WIDTH), row(P_C // SG_WIDTH + 1),
                  pl.BlockSpec((1, SG_WIDTH), lambda i: (0, 0)),
                  pl.BlockSpec((SG_GROUPS, SG_CHUNK, SG_CHUNK), lambda i: (0, 0, 0)),
                  pl.BlockSpec((SG_CHUNK, SG_GROUPS), lambda i: (0, 0))],
        out_specs=row(0),
        out_shape=jax.ShapeDtypeStruct((n, SG_WIDTH), BF16),
        compiler_params=_cparams(("parallel",)),
        name="sg",
    )(proj, proj, g, ws, bst)


def _mix_kernel(ya_ref, yb_ref, yc_ref, ga_ref, gb_ref, gc_ref, wa_ref, wb_ref, wc_ref,
                wo_ref, x_ref, g_ref, o_ref, m_scr, *, nj, tn):
    j = pl.program_id(1)

    def branch(y_ref, w_ref, gate_ref):
        return _sigmoid(gate_ref[...].astype(F32)) * _dot(y_ref[...], w_ref[j])

    merged = branch(ya_ref, wa_ref, ga_ref) + branch(yb_ref, wb_ref, gb_ref) + branch(yc_ref, wc_ref, gc_ref)
    m_scr[j] = merged.astype(BF16)

    @pl.when(j == nj - 1)
    def _():
        acc = _dot(m_scr[0], wo_ref[0:tn, :])
        for jj in range(1, nj):
            acc = acc + _dot(m_scr[jj], wo_ref[jj * tn:(jj + 1) * tn, :])
        o_ref[...] = x_ref[...] + _rms(acc, g_ref[...])


def _mix_out(ya, yb, yc, proj, wa, wb, wc, wo, l, xf, g, tm=512):
    n = xf.shape[0]
    _, nj, _, tn = wa.shape
    yspec = pl.BlockSpec((tm, NA_WIDTH), lambda i, j: (i, 0))
    gate = lambda k: pl.BlockSpec((tm, tn), lambda i, j: (i, (P_G + k * D_MODEL) // tn + j))
    wspec = pl.BlockSpec((None, nj, NA_WIDTH, tn), lambda i, j: (l, 0, 0, 0), pipeline_mode=pl.Buffered(1))
    return pl.pallas_call(
        functools.partial(_mix_kernel, nj=nj, tn=tn),
        grid=(n // tm, nj),
        in_specs=[yspec, yspec, yspec, gate(0), gate(1), gate(2), wspec, wspec, wspec,
                  pl.BlockSpec((None, D_MODEL, D_MODEL), lambda i, j: (l, 0, 0), pipeline_mode=pl.Buffered(1)),
                  pl.BlockSpec((tm, D_MODEL), lambda i, j: (i, 0)),
                  pl.BlockSpec((1, D_MODEL), lambda i, j: (0, 0))],
        out_specs=pl.BlockSpec((tm, D_MODEL), lambda i, j: (i, 0)),
        out_shape=jax.ShapeDtypeStruct((n, D_MODEL), F32),
        scratch_shapes=[pltpu.VMEM((nj, tm, tn), BF16)],
        compiler_params=_cparams(("parallel", "arbitrary")),
        name="mix_out",
    )(ya, yb, yc, proj, proj, proj, wa, wb, wc, wo, xf, g)


def _mlp_kernel(x_ref, g1_ref, wu_ref, wd_ref, g2_ref, o_ref, h_scr, acc_scr, *, nj):
    j = pl.program_id(1)

    @pl.when(j == 0)
    def _():
        h_scr[...] = _rms(x_ref[...], g1_ref[...]).astype(BF16)
        acc_scr[...] = jnp.zeros_like(acc_scr)

    a = jnp.maximum(_dot(h_scr[...], wu_ref[...]), 0.0)
    acc_scr[...] += _dot((a * a).astype(BF16), wd_ref[...])

    @pl.when(j == nj - 1)
    def _():
        o_ref[...] = x_ref[...] + _rms(acc_scr[...], g2_ref[...])


def _mlp(xf, g1, wu, wd, l, g2, tm=512, tf=1024):
    n = xf.shape[0]
    nj = D_FF // tf
    return pl.pallas_call(
        functools.partial(_mlp_kernel, nj=nj),
        grid=(n // tm, nj),
        in_specs=[pl.BlockSpec((tm, D_MODEL), lambda i, j: (i, 0)),
                  pl.BlockSpec((1, D_MODEL), lambda i, j: (0, 0)),
                  pl.BlockSpec((None, D_MODEL, tf), lambda i, j: (l, 0, j)),
                  pl.BlockSpec((None, tf, D_MODEL), lambda i, j: (l, j, 0)),
                  pl.BlockSpec((1, D_MODEL), lambda i, j: (0, 0))],
        out_specs=pl.BlockSpec((tm, D_MODEL), lambda i, j: (i, 0)),
        out_shape=jax.ShapeDtypeStruct((n, D_MODEL), F32),
        scratch_shapes=[pltpu.VMEM((tm, D_MODEL), BF16), pltpu.VMEM((tm, D_MODEL), F32)],
        compiler_params=_cparams(("parallel", "arbitrary")),
        name="mlp",
    )(xf, g1, wu, wd, g2)


def _regroup_kernel(xt_ref, xg_ref, o_ref, gt_ref):
    o_ref[...] = xt_ref[0].T.astype(BF16)

    @pl.when(pl.program_id(1) == 0)
    def _():
        gt_ref[...] = xg_ref[0].astype(BF16)


def _regroup_cast(w_in_t, tc=512):
    depth = w_in_t.shape[0]

    def src_row(l, c):
        r = c * tc
        return (l, pl.multiple_of(jnp.where(r < OFF_BG, r, r + N_GATE), N_GATE), 0)

    return pl.pallas_call(
        _regroup_kernel,
        grid=(depth, P_WIDTH // tc),
        in_specs=[pl.BlockSpec((pl.Element(1), pl.Element(tc), pl.Element(D_MODEL)), src_row),
                  pl.BlockSpec((pl.Element(1), pl.Element(N_GATE), pl.Element(D_MODEL)),
                               lambda l, c: (l, OFF_BG, 0))],
        out_specs=[pl.BlockSpec((None, D_MODEL, tc), lambda l, c: (l, 0, c)),
                   pl.BlockSpec((None, N_GATE, D_MODEL), lambda l, c: (l, 0, 0))],
        out_shape=[jax.ShapeDtypeStruct((depth, D_MODEL, P_WIDTH), BF16),
                   jax.ShapeDtypeStruct((depth, N_GATE, D_MODEL), BF16)],
        compiler_params=_cparams(("parallel", "arbitrary")),
        name="regroup_cast",
    )(w_in_t, w_in_t)


def _col_tiles(w, tn=MIX_TN):
    d, k, n = w.shape
    return w.astype(BF16).reshape(d, k, n // tn, tn).transpose(0, 2, 1, 3)


def _prep_shared(w_in, w_a, w_b, w_c, w_out, w_up, w_down):
    w_main, w_gate_t = _regroup_cast(jnp.swapaxes(w_in, 1, 2))
    return dict(w_main=w_main, w_gate_t=w_gate_t,
                w_a=_col_tiles(w_a), w_b=_col_tiles(w_b), w_c=_col_tiles(w_c),
                w_out=w_out.astype(BF16), w_up=w_up.astype(BF16), w_down=w_down.astype(BF16))


def _prep_layer(l, shared, pre_mix_g, post_mix_g, pre_mlp_g, post_mlp_g, b_gate, conv_w, na_rpb,
                ml_norm_g, sg_norm_g, w_s, b_s):
    w_gt = shared["w_gate_t"][l]
    p = dict(shared)
    p.update(
        l=l,
        pre_mix_g=pre_mix_g[l].reshape(1, D_MODEL), post_mix_g=post_mix_g[l].reshape(1, D_MODEL),
        pre_mlp_g=pre_mlp_g[l].reshape(1, D_MODEL), post_mlp_g=post_mlp_g[l].reshape(1, D_MODEL),
        w_gate=jnp.pad(w_gt.T, ((0, 0), (0, LANE - N_GATE))), w_gate_t=w_gt,
        bg_col=jnp.pad(b_gate[l].astype(F32), (0, LANE - N_GATE)).reshape(1, LANE),
        bg_row=b_gate[l].astype(F32).reshape(N_GATE, 1),
        conv_w=conv_w[l].astype(F32), bias_tab=_na_bias_table(na_rpb[l]),
        ml_norm_g=ml_norm_g[l].reshape(1, ML_WIDTH), sg_norm_g=sg_norm_g[l].reshape(1, SG_WIDTH),
        w_s=w_s[l].astype(BF16), bst=b_s[l].astype(F32).T,
    )
    return p


def _trunk_layer(x, p):
    b, t, _ = x.shape
    n = b * t
    l = p["l"]
    xf = x.reshape(n, D_MODEL)
    proj, gate_col, gate_row = _in_proj(xf, p["pre_mix_g"], p["w_main"], l, p["w_gate"], p["w_gate_t"],
                                        p["bg_col"], p["bg_row"])
    proj3 = proj.reshape(b, t, P_WIDTH)
    y_a = _na(proj3, p["bias_tab"]).reshape(n, NA_WIDTH)
    h_f, qs, ks = _mlstm_fwd(proj3, gate_col, gate_row, p["conv_w"])
    y_b = _mlstm_bwd(qs, ks, proj3, gate_col, gate_row, h_f, p["ml_norm_g"]).reshape(n, ML_WIDTH)
    y_c = _sg(proj, p["sg_norm_g"], p["w_s"], p["bst"])
    x1 = _mix_out(y_a, y_b, y_c, proj, p["w_a"], p["w_b"], p["w_c"], p["w_out"], l, xf, p["post_mix_g"])
    x2 = _mlp(x1, p["pre_mlp_g"], p["w_up"], p["w_down"], l, p["post_mlp_g"])
    return x2.reshape(b, t, D_MODEL)


def kernel(x_prompt, x_sample, pre_mix_g, post_mix_g, pre_mlp_g, post_mlp_g, w_in, b_gate, conv_w,
           na_rpb, ml_norm_g, sg_norm_g, w_s, b_s, w_a, w_b, w_c, w_out, w_up, w_down):
    shared = _prep_shared(w_in, w_a, w_b, w_c, w_out, w_up, w_down)
    layers = [_prep_layer(l, shared, pre_mix_g, post_mix_g, pre_mlp_g, post_mlp_g, b_gate, conv_w,
                          na_rpb, ml_norm_g, sg_norm_g, w_s, b_s)
              for l in range(w_in.shape[0])]

    def run(x):
        for p in layers:
            x = _trunk_layer(x, p)
        return x

    return (run(x_prompt), run(x_sample))
```

```python
import functools
import math

import numpy as np
import jax
import jax.numpy as jnp
from jax import lax
from jax.experimental import pallas as pl
from jax.experimental.pallas import tpu as pltpu

F32 = jnp.float32
BF16 = jnp.bfloat16

D_MODEL = 2048
GRID_W = 64
EPS = 1e-6

NA_HEADS = 8
NA_HEAD_DIM = 128
NA_WIDTH = NA_HEADS * NA_HEAD_DIM
NA_KR = 8
NA_KC = 16
NA_ROWS_PER_STEP = 16

ML_HEADS = 4
ML_HEAD_DIM = 256
ML_WIDTH = ML_HEADS * ML_HEAD_DIM
ML_CHUNK = 256
ML_STRIP = 256
MIX_TN = 1024

SG_GROUPS = 8
SG_CHUNK = 128
SG_WIDTH = 1024
SG_GROUP_DIM = SG_WIDTH // SG_GROUPS

D_FF = 4 * D_MODEL
N_BRANCH = 3

OFF_A = 0
OFF_B = OFF_A + 3 * NA_WIDTH
OFF_BG = OFF_B + 4 * ML_WIDTH
OFF_C = OFF_BG + 4 * ML_HEADS
OFF_G = OFF_C + 2 * SG_WIDTH
D_IN = OFF_G + N_BRANCH * D_MODEL
N_GATE = 4 * ML_HEADS

P_A = 0
P_B = P_A + 3 * NA_WIDTH
P_C = P_B + 4 * ML_WIDTH
P_G = P_C + 2 * SG_WIDTH
P_WIDTH = P_G + N_BRANCH * D_MODEL

LANE = 128
HALO_ROWS = 16
VMEM_LIMIT = 56 * 1024 * 1024
BIG_VMEM_LIMIT = 60 * 1024 * 1024
LOG2E = np.float32(1.4426950408889634)


def _cparams(sem, vmem=VMEM_LIMIT, **kw):
    return pltpu.CompilerParams(dimension_semantics=sem, vmem_limit_bytes=vmem, **kw)


def _rms(x, g):
    return x * lax.rsqrt(jnp.mean(x * x, axis=-1, keepdims=True) + EPS) * g


def _sigmoid(x):
    return 1.0 / (1.0 + jnp.exp(-x))


def _log_sigmoid(x):
    return jnp.minimum(x, 0.0) - jnp.log(1.0 + jnp.exp(-jnp.abs(x)))


def _gelu_tanh(x):
    k = -2.0 * np.sqrt(2.0 / np.pi) * float(LOG2E)
    z = x * (np.float32(k * 0.044715) * (x * x) + np.float32(k))
    return x * (1.0 / (1.0 + jnp.exp2(z)))


def _dot(a, b):
    return jnp.dot(a, b, preferred_element_type=F32)


def _dot_nt(a, b):
    return lax.dot_general(a, b, (((1,), (1,)), ((), ())), preferred_element_type=F32)


def _dot_tn(a, b):
    return lax.dot_general(a, b, (((0,), (0,)), ((), ())), preferred_element_type=F32)


def _inproj_kernel(x_ref, g_ref, w_ref, wg_ref, wgt_ref, bgc_ref, bgr_ref,
                   o_ref, gc_ref, gr_ref, h_scr):
    @pl.when(pl.program_id(1) == 0)
    def _():
        hb = _rms(x_ref[...], g_ref[...]).astype(BF16)
        h_scr[...] = hb
        gc_ref[...] = _dot(hb, wg_ref[...]) + bgc_ref[...]
        gr_ref[...] = _dot_nt(wgt_ref[...], hb) + bgr_ref[...]

    o_ref[...] = _dot(h_scr[...], w_ref[...]).astype(BF16)


def _in_proj(xf, g, w_main, l, w_gate, w_gate_t, bg_col, bg_row, tm=1024, tn=1536):
    n = xf.shape[0]
    tm = min(tm, n)
    grid = (n // tm, P_WIDTH // tn)
    return pl.pallas_call(
        _inproj_kernel,
        grid=grid,
        in_specs=[
            pl.BlockSpec((tm, D_MODEL), lambda i, j: (i, 0)),
            pl.BlockSpec((1, D_MODEL), lambda i, j: (0, 0)),
            pl.BlockSpec((None, D_MODEL, tn), lambda i, j: (l, 0, j)),
            pl.BlockSpec((D_MODEL, LANE), lambda i, j: (0, 0)),
            pl.BlockSpec((N_GATE, D_MODEL), lambda i, j: (0, 0)),
            pl.BlockSpec((1, LANE), lambda i, j: (0, 0)),
            pl.BlockSpec((N_GATE, 1), lambda i, j: (0, 0)),
        ],
        out_specs=[
            pl.BlockSpec((tm, tn), lambda i, j: (i, j)),
            pl.BlockSpec((tm, LANE), lambda i, j: (i, 0)),
            pl.BlockSpec((N_GATE, tm), lambda i, j: (0, i)),
        ],
        out_shape=[
            jax.ShapeDtypeStruct((n, P_WIDTH), BF16),
            jax.ShapeDtypeStruct((n, LANE), F32),
            jax.ShapeDtypeStruct((N_GATE, n), F32),
        ],
        scratch_shapes=[pltpu.VMEM((tm, D_MODEL), BF16)],
        compiler_params=_cparams(("parallel", "arbitrary")),
        name="in_proj",
    )(xf, g, w_main, w_gate, w_gate_t, bg_col, bg_row)


def _na_bias_table(rpb):
    qc = np.arange(GRID_W)[:, None]
    kc = np.arange(GRID_W)[None, :]
    cs = np.clip(qc - NA_KC // 2, 0, GRID_W - NA_KC)
    ok = (kc >= cs) & (kc < cs + NA_KC)
    dc = np.clip(kc - qc + NA_KC - 1, 0, 2 * NA_KC - 2)
    onehot = (dc[None] == np.arange(2 * NA_KC - 1)[:, None, None]).astype(np.float32)
    col = jnp.einsum("hrc,cqk->hrqk", rpb.astype(F32), onehot, precision=lax.Precision.HIGHEST)
    col = jnp.where(jnp.asarray(ok)[None, None], col * LOG2E, -1e30)
    tab = jnp.stack([col[:, d:d + NA_KR] for d in range(NA_KR)], axis=1)
    tab = tab.transpose(0, 1, 3, 2, 4)
    return tab.reshape(NA_HEADS, NA_KR, GRID_W, NA_KR * GRID_W)


def _na_kernel(q_ref, k_ref, v_ref, bias_ref, o_ref, *, rows, rps):
    scale = np.float32(NA_HEAD_DIM ** -0.5 * float(LOG2E))
    win = NA_KR * GRID_W

    def body(i, carry):
        rr = [i * rps + u for u in range(rps)]
        rs = [jnp.clip(r - NA_KR // 2, 0, rows - NA_KR) for r in rr]
        q0 = [pl.multiple_of(r * GRID_W, GRID_W) for r in rr]
        k0 = [pl.multiple_of(x * GRID_W, GRID_W) for x in rs]
        s = [_dot_nt(q_ref[pl.ds(q0[u], GRID_W), :], k_ref[pl.ds(k0[u], win), :]) for u in range(rps)]
        p, inv = [], []
        for u in range(rps):
            su = s[u] * scale + bias_ref[rs[u] - rr[u] + NA_KR - 1]
            e = jnp.exp2(su - jnp.max(su, axis=-1, keepdims=True))
            inv.append(1.0 / jnp.sum(e, axis=-1, keepdims=True))
            p.append(e.astype(BF16))
        for u in range(rps):
            o = _dot(p[u], v_ref[pl.ds(k0[u], win), :]) * inv[u]
            o_ref[pl.ds(q0[u], GRID_W), :] = o.astype(BF16)
        return carry

    lax.fori_loop(0, rows // rps, body, 0)


def _na(proj3, bias_tab):
    b, t, _ = proj3.shape
    rows = t // GRID_W
    blk = lambda off: pl.BlockSpec((None, t, NA_HEAD_DIM), lambda bi, h: (bi, 0, off + h))
    return pl.pallas_call(
        functools.partial(_na_kernel, rows=rows, rps=math.gcd(rows, NA_ROWS_PER_STEP)),
        grid=(b, NA_HEADS),
        in_specs=[
            blk(P_A // NA_HEAD_DIM),
            blk(P_A // NA_HEAD_DIM + NA_HEADS),
            blk(P_A // NA_HEAD_DIM + 2 * NA_HEADS),
            pl.BlockSpec((None, NA_KR, GRID_W, NA_KR * GRID_W), lambda bi, h: (h, 0, 0, 0)),
        ],
        out_specs=pl.BlockSpec((None, t, NA_HEAD_DIM), lambda bi, h: (bi, 0, h)),
        out_shape=jax.ShapeDtypeStruct((b, t, NA_WIDTH), BF16),
        compiler_params=_cparams(("parallel", "parallel")),
        name="na",
    )(proj3, proj3, proj3, bias_tab)


def _tri_product(tri_b, x, *, left):
    hi = x.astype(BF16)
    rest = x - hi.astype(F32)
    mid = rest.astype(BF16)
    lo = (rest - mid.astype(F32)).astype(BF16)
    if left:
        return (_dot(tri_b, lo) + _dot(tri_b, mid)) + _dot(tri_b, hi)
    return (_dot(lo, tri_b) + _dot(mid, tri_b)) + _dot(hi, tri_b)


def _mlstm_init(c_scr, n_scr, m_scr):
    @pl.when(pl.program_id(1) == 0)
    def _():
        c_scr[...] = jnp.zeros_like(c_scr)
        n_scr[...] = jnp.zeros_like(n_scr)
        m_scr[...] = jnp.zeros_like(m_scr)


def _mlstm_chunk(q_of, k_of, v_ref, gc_ref, gr_ref, c_scr, n_scr, m_scr, emit, *, reverse, L):
    ri = lax.broadcasted_iota(jnp.int32, (L, L), 0)
    ci = lax.broadcasted_iota(jnp.int32, (L, L), 1)
    mask = (ci >= ri) if reverse else (ci <= ri)
    mask_t = (ri >= ci) if reverse else (ri <= ci)
    gi = 2 if reverse else 0
    last = 0 if reverse else L - 1
    R = ML_STRIP

    gcol = gc_ref[...]
    grow = gr_ref[...]
    b_col_all = _tri_product(jnp.where(mask, 1.0, 0.0).astype(BF16), _log_sigmoid(gcol), left=True)
    b_row_all = _tri_product(jnp.where(mask_t, 1.0, 0.0).astype(BF16), _log_sigmoid(grow), left=False)
    b2_col_all = b_col_all * LOG2E

    heads = range(ML_HEADS)
    hsl = [slice(hd * ML_HEAD_DIM, (hd + 1) * ML_HEAD_DIM) for hd in heads]
    ii = [gi * ML_HEADS + hd for hd in heads]
    fi = [(gi + 1) * ML_HEADS + hd for hd in heads]
    m_prev = [m_scr[hd][:, 0:1] for hd in heads]
    c_prev = [c_scr[hd] for hd in heads]
    n_prev = [n_scr[hd] for hd in heads]
    q = [q_of(hd) for hd in heads]
    k = [k_of(hd) for hd in heads]
    vb = [v_ref[:, hsl[hd]] for hd in heads]

    strips = []
    for st in range(L // R):
        if reverse:
            c0, c1 = (st * R) // LANE * LANE, L
        else:
            c0, c1 = 0, -(-((st + 1) * R) // LANE) * LANE
        strips.append((st, slice(st * R, (st + 1) * R), slice(c0, c1)))
    units = [(hd, st, rows, cols) for hd in heads for (st, rows, cols) in strips]

    qk = [_dot_nt(q[hd][1][rows], k[hd][1][cols]) for (hd, st, rows, cols) in units]
    qc = [_dot(q[hd][1][rows], c_prev[hd].astype(BF16)) for (hd, st, rows, cols) in units]

    sb, scale_c, inv = [], [], []
    for u, (hd, st, rows, cols) in enumerate(units):
        c0, c1 = cols.start, cols.stop
        b2_col = b2_col_all[rows, fi[hd]:fi[hd] + 1]
        r2_row = (grow[ii[hd]:ii[hd] + 1, cols] - b_row_all[fi[hd]:fi[hd] + 1, cols]) * LOG2E
        sri = lax.broadcasted_iota(jnp.int32, (R, c1 - c0), 0) + st * R
        sci = lax.broadcasted_iota(jnp.int32, (R, c1 - c0), 1) + c0
        smask = (sci >= sri) if reverse else (sci <= sri)
        d2 = jnp.where(smask, b2_col + r2_row, -1e30)
        inter2 = b2_col + m_prev[hd] * LOG2E
        m2 = jnp.maximum(inter2, jnp.max(d2, axis=1, keepdims=True))
        s = qk[u] * jnp.exp2(d2 - m2)
        w_inter = jnp.exp2(inter2 - m2)
        nq = (jnp.sum(s, axis=1, keepdims=True)
              + w_inter * jnp.sum(q[hd][0][rows] * n_prev[hd], axis=1, keepdims=True))
        r = 1.0 / jnp.maximum(jnp.abs(nq), jnp.exp2(-m2))
        sb.append(s.astype(BF16))
        scale_c.append(w_inter * r)
        inv.append(r)

    for u, (hd, st, rows, cols) in enumerate(units):
        emit(hd, rows, _dot(sb[u], vb[hd][cols]) * inv[u] + qc[u] * scale_c[u])

    kws, decays, m_news = [], [], []
    for hd in heads:
        b_col = b_col_all[:, fi[hd]:fi[hd] + 1]
        b_last = b_col_all[last:last + 1, fi[hd]:fi[hd] + 1]
        wlog = b_last - b_col + gcol[:, ii[hd]:ii[hd] + 1]
        m_new = jnp.maximum(b_last + m_prev[hd], jnp.max(wlog, axis=0, keepdims=True))
        kws.append(k[hd][0] * jnp.exp(wlog - m_new))
        decays.append(jnp.exp(b_last + m_prev[hd] - m_new))
        m_news.append(m_new)
    eye = jnp.where(lax.broadcasted_iota(jnp.int32, (ML_HEAD_DIM, ML_HEAD_DIM), 0)
                    == lax.broadcasted_iota(jnp.int32, (ML_HEAD_DIM, ML_HEAD_DIM), 1), 1.0, 0.0).astype(BF16)
    kwt = [_dot_nt(eye, kws[hd].astype(BF16)).astype(BF16) for hd in heads]
    for hd in heads:
        c_scr[hd] = decays[hd] * c_prev[hd] + _dot(kwt[hd], vb[hd])
        n_scr[hd] = decays[hd] * n_prev[hd] + jnp.sum(kws[hd], axis=0, keepdims=True)
        m_scr[hd] = jnp.broadcast_to(m_news[hd], (1, LANE))


def _mlstm_fwd_kernel(qp_ref, q_ref, qn_ref, kp_ref, k_ref, kn_ref, v_ref, gc_ref, gr_ref, cw_ref,
                      h_ref, qs_ref, ks_ref, c_scr, n_scr, m_scr, *, nc, L):
    c = pl.program_id(1)
    _mlstm_init(c_scr, n_scr, m_scr)

    rowi = lax.broadcasted_iota(jnp.int32, (L, 1), 0)
    has_prev = (c > 0).astype(F32)
    has_next = (c < nc - 1).astype(F32)

    def conv_silu(p_ref, x_ref, n_ref, w):
        x = x_ref[...].astype(F32)
        prev_row = p_ref[HALO_ROWS - 1:HALO_ROWS, :].astype(F32) * has_prev
        next_row = n_ref[0:1, :].astype(F32) * has_next
        xp = jnp.where(rowi == 0, prev_row, pltpu.roll(x, 1, 0))
        xn = jnp.where(rowi == L - 1, next_row, pltpu.roll(x, L - 1, 0))
        y = xp * w[0:1] + x * w[1:2] + xn * w[2:3]
        return y * _sigmoid(y)

    cw = cw_ref[...]
    qs = conv_silu(qp_ref, q_ref, qn_ref, cw[:, :ML_WIDTH]) * np.float32(ML_HEAD_DIM ** -0.5)
    ks = conv_silu(kp_ref, k_ref, kn_ref, cw[:, ML_WIDTH:])
    qsb = qs.astype(BF16)
    ksb = ks.astype(BF16)
    qs_ref[...] = qsb
    ks_ref[...] = ksb

    def head(x, xb):
        return lambda hd: (x[:, hd * ML_HEAD_DIM:(hd + 1) * ML_HEAD_DIM],
                           xb[:, hd * ML_HEAD_DIM:(hd + 1) * ML_HEAD_DIM])

    def emit(hd, rows, h):
        h_ref[rows, hd * ML_HEAD_DIM:(hd + 1) * ML_HEAD_DIM] = h

    _mlstm_chunk(head(qs, qsb), head(ks, ksb), v_ref, gc_ref, gr_ref, c_scr, n_scr, m_scr, emit,
                 reverse=False, L=L)


def _mlstm_bwd_kernel(qs_ref, ks_ref, v_ref, gc_ref, gr_ref, hf_ref, o_ref, g_ref,
                      y_ref, c_scr, n_scr, m_scr, *, L):
    _mlstm_init(c_scr, n_scr, m_scr)

    def head(x_ref):
        def get(hd):
            xb = x_ref[:, hd * ML_HEAD_DIM:(hd + 1) * ML_HEAD_DIM]
            return xb.astype(F32), xb
        return get

    def emit(hd, rows, h):
        sl = slice(hd * ML_HEAD_DIM, (hd + 1) * ML_HEAD_DIM)
        y = _rms(h + hf_ref[rows, sl], g_ref[:, sl]) * _sigmoid(o_ref[rows, sl].astype(F32))
        y_ref[rows, sl] = y.astype(BF16)

    _mlstm_chunk(head(qs_ref), head(ks_ref), v_ref, gc_ref, gr_ref, c_scr, n_scr, m_scr, emit,
                 reverse=True, L=L)


def _mlstm_scratch():
    return [pltpu.VMEM((ML_HEADS, ML_HEAD_DIM, ML_HEAD_DIM), F32),
            pltpu.VMEM((ML_HEADS, 1, ML_HEAD_DIM), F32),
            pltpu.VMEM((ML_HEADS, 1, LANE), F32)]


def _mlstm_fwd(proj3, gate_col, gate_row, conv_w):
    b, t, _ = proj3.shape
    L = ML_CHUNK
    nc = t // L
    hb = L // HALO_ROWS
    n_halo = t // HALO_ROWS
    cur = lambda col: pl.BlockSpec((None, L, ML_WIDTH), lambda bi, c: (bi, c, col))
    prv = lambda col: pl.BlockSpec((None, HALO_ROWS, ML_WIDTH),
                                   lambda bi, c: (bi, jnp.maximum(c * hb - 1, 0), col))
    nxt = lambda col: pl.BlockSpec((None, HALO_ROWS, ML_WIDTH),
                                   lambda bi, c: (bi, jnp.minimum((c + 1) * hb, n_halo - 1), col))
    qc, kc, vc = P_B // ML_WIDTH, P_B // ML_WIDTH + 1, P_B // ML_WIDTH + 2
    return pl.pallas_call(
        functools.partial(_mlstm_fwd_kernel, nc=nc, L=L),
        grid=(b, nc),
        in_specs=[
            prv(qc), cur(qc), nxt(qc),
            prv(kc), cur(kc), nxt(kc),
            cur(vc),
            pl.BlockSpec((L, LANE), lambda bi, c: (bi * nc + c, 0)),
            pl.BlockSpec((N_GATE, L), lambda bi, c: (0, bi * nc + c)),
            pl.BlockSpec((3, 2 * ML_WIDTH), lambda bi, c: (0, 0)),
        ],
        out_specs=[cur(0), cur(0), cur(0)],
        out_shape=[jax.ShapeDtypeStruct((b, t, ML_WIDTH), F32),
                   jax.ShapeDtypeStruct((b, t, ML_WIDTH), BF16),
                   jax.ShapeDtypeStruct((b, t, ML_WIDTH), BF16)],
        scratch_shapes=_mlstm_scratch(),
        compiler_params=_cparams(("parallel", "arbitrary")),
        name="mlstm_fwd",
    )(proj3, proj3, proj3, proj3, proj3, proj3, proj3, gate_col, gate_row, conv_w)


def _mlstm_bwd(qs, ks, proj3, gate_col, gate_row, h_f, g):
    b, t, _ = proj3.shape
    L = ML_CHUNK
    nc = t // L
    cur = lambda col: pl.BlockSpec((None, L, ML_WIDTH), lambda bi, c: (bi, nc - 1 - c, col))
    vc, oc = P_B // ML_WIDTH + 2, P_B // ML_WIDTH + 3
    return pl.pallas_call(
        functools.partial(_mlstm_bwd_kernel, L=L),
        grid=(b, nc),
        in_specs=[
            cur(0), cur(0), cur(vc),
            pl.BlockSpec((L, LANE), lambda bi, c: (bi * nc + nc - 1 - c, 0)),
            pl.BlockSpec((N_GATE, L), lambda bi, c: (0, bi * nc + nc - 1 - c)),
            cur(0), cur(oc),
            pl.BlockSpec((1, ML_WIDTH), lambda bi, c: (0, 0)),
        ],
        out_specs=cur(0),
        out_shape=jax.ShapeDtypeStruct((b, t, ML_WIDTH), BF16),
        scratch_shapes=_mlstm_scratch(),
        compiler_params=_cparams(("parallel", "arbitrary")),
        name="mlstm_bwd",
    )(qs, ks, proj3, gate_col, gate_row, h_f, proj3, g)


def _sg_kernel(u_ref, v_ref, g_ref, ws_ref, bst_ref, y_ref, *, chunks):
    for cidx in range(chunks):
        rs = slice(cidx * SG_CHUNK, (cidx + 1) * SG_CHUNK)
        zu = _gelu_tanh(u_ref[rs, :].astype(F32))
        zv = _gelu_tanh(v_ref[rs, :].astype(F32))
        vb = _rms(zv, g_ref[...]).astype(BF16)
        for gidx in range(SG_GROUPS):
            cs = slice(gidx * SG_GROUP_DIM, (gidx + 1) * SG_GROUP_DIM)
            mixed = _dot(ws_ref[gidx], vb[:, cs]) + bst_ref[:, gidx:gidx + 1]
            y_ref[rs, cs] = (zu[:, cs] * mixed).astype(BF16)


def _sg(proj, g, ws, bst, chunks=4):
    n = proj.shape[0]
    tm = chunks * SG_CHUNK
    row = lambda col: pl.BlockSpec((tm, SG_WIDTH), lambda i: (i, col))
    return pl.pallas_call(
        functools.partial(_sg_kernel, chunks=chunks),
        grid=(n // tm,),
        in_specs=[row(P_C // SG_WIDTH), row(P_C // SG_WIDTH + 1),
                  pl.BlockSpec((1, SG_WIDTH), lambda i: (0, 0)),
                  pl.BlockSpec((SG_GROUPS, SG_CHUNK, SG_CHUNK), lambda i: (0, 0, 0)),
                  pl.BlockSpec((SG_CHUNK, SG_GROUPS), lambda i: (0, 0))],
        out_specs=row(0),
        out_shape=jax.ShapeDtypeStruct((n, SG_WIDTH), BF16),
        compiler_params=_cparams(("parallel",)),
        name="sg",
    )(proj, proj, g, ws, bst)


def _mix_kernel(ya_ref, yb_ref, yc_ref, ga_ref, gb_ref, gc_ref, wa_ref, wb_ref, wc_ref,
                wo_ref, x_ref, g_ref, o_ref, m_scr, *, nj, tn):
    j = pl.program_id(1)

    def branch(y_ref, w_ref, gate_ref):
        return _sigmoid(gate_ref[...].astype(F32)) * _dot(y_ref[...], w_ref[j])

    merged = branch(ya_ref, wa_ref, ga_ref) + branch(yb_ref, wb_ref, gb_ref) + branch(yc_ref, wc_ref, gc_ref)
    m_scr[j] = merged.astype(BF16)

    @pl.when(j == nj - 1)
    def _():
        acc = _dot(m_scr[0], wo_ref[0:tn, :])
        for jj in range(1, nj):
            acc = acc + _dot(m_scr[jj], wo_ref[jj * tn:(jj + 1) * tn, :])
        o_ref[...] = x_ref[...] + _rms(acc, g_ref[...])


def _mix_out(ya, yb, yc, proj, wa, wb, wc, wo, l, xf, g, tm=512):
    n = xf.shape[0]
    _, nj, _, tn = wa.shape
    yspec = pl.BlockSpec((tm, NA_WIDTH), lambda i, j: (i, 0))
    gate = lambda k: pl.BlockSpec((tm, tn), lambda i, j: (i, (P_G + k * D_MODEL) // tn + j))
    wspec = pl.BlockSpec((None, nj, NA_WIDTH, tn), lambda i, j: (l, 0, 0, 0), pipeline_mode=pl.Buffered(1))
    return pl.pallas_call(
        functools.partial(_mix_kernel, nj=nj, tn=tn),
        grid=(n // tm, nj),
        in_specs=[yspec, yspec, yspec, gate(0), gate(1), gate(2), wspec, wspec, wspec,
                  pl.BlockSpec((None, D_MODEL, D_MODEL), lambda i, j: (l, 0, 0), pipeline_mode=pl.Buffered(1)),
                  pl.BlockSpec((tm, D_MODEL), lambda i, j: (i, 0)),
                  pl.BlockSpec((1, D_MODEL), lambda i, j: (0, 0))],
        out_specs=pl.BlockSpec((tm, D_MODEL), lambda i, j: (i, 0)),
        out_shape=jax.ShapeDtypeStruct((n, D_MODEL), F32),
        scratch_shapes=[pltpu.VMEM((nj, tm, tn), BF16)],
        compiler_params=_cparams(("parallel", "arbitrary"), vmem=BIG_VMEM_LIMIT),
        name="mix_out",
    )(ya, yb, yc, proj, proj, proj, wa, wb, wc, wo, xf, g)


def _mlp_kernel(x_ref, g1_ref, wu_ref, wd_ref, g2_ref, o_ref, h_scr, *, nj, tf, sub):
    j = pl.program_id(1)

    @pl.when(j == 0)
    def _():
        h_scr[...] = _rms(x_ref[...], g1_ref[...]).astype(BF16)
        o_ref[...] = jnp.zeros_like(o_ref)

    acc = o_ref[...]
    for s in range(tf // sub):
        a = jnp.maximum(_dot(h_scr[...], wu_ref[:, s * sub:(s + 1) * sub]), 0.0)
        acc = acc + _dot((a * a).astype(BF16), wd_ref[s * sub:(s + 1) * sub, :])
    o_ref[...] = acc

    @pl.when(j == nj - 1)
    def _():
        o_ref[...] = x_ref[...] + _rms(o_ref[...], g2_ref[...])


def _mlp(xf, g1, wu, wd, l, g2, tm=512, tf=2048, sub=1024):
    n = xf.shape[0]
    nj = D_FF // tf
    return pl.pallas_call(
        functools.partial(_mlp_kernel, nj=nj, tf=tf, sub=sub),
        grid=(n // tm, nj),
        in_specs=[pl.BlockSpec((tm, D_MODEL), lambda i, j: (i, 0)),
                  pl.BlockSpec((1, D_MODEL), lambda i, j: (0, 0)),
                  pl.BlockSpec((None, D_MODEL, tf), lambda i, j: (l, 0, j)),
                  pl.BlockSpec((None, tf, D_MODEL), lambda i, j: (l, j, 0)),
                  pl.BlockSpec((1, D_MODEL), lambda i, j: (0, 0))],
        out_specs=pl.BlockSpec((tm, D_MODEL), lambda i, j: (i, 0)),
        out_shape=jax.ShapeDtypeStruct((n, D_MODEL), F32),
        scratch_shapes=[pltpu.VMEM((tm, D_MODEL), BF16)],
        compiler_params=_cparams(("parallel", "arbitrary"), vmem=BIG_VMEM_LIMIT),
        name="mlp",
    )(xf, g1, wu, wd, g2)


def _regroup_kernel(xt_ref, xg_ref, o_ref, gt_ref):
    o_ref[...] = xt_ref[0].T.astype(BF16)

    @pl.when(pl.program_id(1) == 0)
    def _():
        gt_ref[...] = xg_ref[0].astype(BF16)


def _regroup_cast(w_in_t, tc=512):
    depth = w_in_t.shape[0]

    def src_row(l, c):
        r = c * tc
        return (l, pl.multiple_of(jnp.where(r < OFF_BG, r, r + N_GATE), N_GATE), 0)

    return pl.pallas_call(
        _regroup_kernel,
        grid=(depth, P_WIDTH // tc),
        in_specs=[pl.BlockSpec((pl.Element(1), pl.Element(tc), pl.Element(D_MODEL)), src_row),
                  pl.BlockSpec((pl.Element(1), pl.Element(N_GATE), pl.Element(D_MODEL)),
                               lambda l, c: (l, OFF_BG, 0))],
        out_specs=[pl.BlockSpec((None, D_MODEL, tc), lambda l, c: (l, 0, c)),
                   pl.BlockSpec((None, N_GATE, D_MODEL), lambda l, c: (l, 0, 0))],
        out_shape=[jax.ShapeDtypeStruct((depth, D_MODEL, P_WIDTH), BF16),
                   jax.ShapeDtypeStruct((depth, N_GATE, D_MODEL), BF16)],
        compiler_params=_cparams(("parallel", "arbitrary")),
        name="regroup_cast",
    )(w_in_t, w_in_t)


def _col_tiles(w, tn=MIX_TN):
    d, k, n = w.shape
    return w.astype(BF16).reshape(d, k, n // tn, tn).transpose(0, 2, 1, 3)


def _prep_shared(w_in, w_a, w_b, w_c, w_out, w_up, w_down):
    w_main, w_gate_t = _regroup_cast(jnp.swapaxes(w_in, 1, 2))
    return dict(w_main=w_main, w_gate_t=w_gate_t,
                w_a=_col_tiles(w_a), w_b=_col_tiles(w_b), w_c=_col_tiles(w_c),
                w_out=w_out.astype(BF16), w_up=w_up.astype(BF16), w_down=w_down.astype(BF16))


def _prep_layer(l, shared, pre_mix_g, post_mix_g, pre_mlp_g, post_mlp_g, b_gate, conv_w, na_rpb,
                ml_norm_g, sg_norm_g, w_s, b_s):
    w_gt = shared["w_gate_t"][l]
    p = dict(shared)
    p.update(
        l=l,
        pre_mix_g=pre_mix_g[l].reshape(1, D_MODEL), post_mix_g=post_mix_g[l].reshape(1, D_MODEL),
        pre_mlp_g=pre_mlp_g[l].reshape(1, D_MODEL), post_mlp_g=post_mlp_g[l].reshape(1, D_MODEL),
        w_gate=jnp.pad(w_gt.T, ((0, 0), (0, LANE - N_GATE))), w_gate_t=w_gt,
        bg_col=jnp.pad(b_gate[l].astype(F32), (0, LANE - N_GATE)).reshape(1, LANE),
        bg_row=b_gate[l].astype(F32).reshape(N_GATE, 1),
        conv_w=conv_w[l].astype(F32), bias_tab=_na_bias_table(na_rpb[l]),
        ml_norm_g=ml_norm_g[l].reshape(1, ML_WIDTH), sg_norm_g=sg_norm_g[l].reshape(1, SG_WIDTH),
        w_s=w_s[l].astype(BF16), bst=b_s[l].astype(F32).T,
    )
    return p


def _trunk_layer(x, p):
    b, t, _ = x.shape
    n = b * t
    l = p["l"]
    xf = x.reshape(n, D_MODEL)
    proj, gate_col, gate_row = _in_proj(xf, p["pre_mix_g"], p["w_main"], l, p["w_gate"], p["w_gate_t"],
                                        p["bg_col"], p["bg_row"])
    proj3 = proj.reshape(b, t, P_WIDTH)
    y_a = _na(proj3, p["bias_tab"]).reshape(n, NA_WIDTH)
    h_f, qs, ks = _mlstm_fwd(proj3, gate_col, gate_row, p["conv_w"])
    y_b = _mlstm_bwd(qs, ks, proj3, gate_col, gate_row, h_f, p["ml_norm_g"]).reshape(n, ML_WIDTH)
    y_c = _sg(proj, p["sg_norm_g"], p["w_s"], p["bst"])
    x1 = _mix_out(y_a, y_b, y_c, proj, p["w_a"], p["w_b"], p["w_c"], p["w_out"], l, xf, p["post_mix_g"])
    x2 = _mlp(x1, p["pre_mlp_g"], p["w_up"], p["w_down"], l, p["post_mlp_g"])
    return x2.reshape(b, t, D_MODEL)


def kernel(x_prompt, x_sample, pre_mix_g, post_mix_g, pre_mlp_g, post_mlp_g, w_in, b_gate, conv_w,
           na_rpb, ml_norm_g, sg_norm_g, w_s, b_s, w_a, w_b, w_c, w_out, w_up, w_down):
    shared = _prep_shared(w_in, w_a, w_b, w_c, w_out, w_up, w_down)
    layers = [_prep_layer(l, shared, pre_mix_g, post_mix_g, pre_mlp_g, post_mlp_g, b_gate, conv_w,
                          na_rpb, ml_norm_g, sg_norm_g, w_s, b_s)
              for l in range(w_in.shape[0])]

    def run(x):
        for p in layers:
            x = _trunk_layer(x, p)
        return x

    return (run(x_prompt), run(x_sample))
```

```python
import functools
import math

import numpy as np
import jax
import jax.numpy as jnp
from jax import lax
from jax.experimental import pallas as pl
from jax.experimental.pallas import tpu as pltpu

F32 = jnp.float32
BF16 = jnp.bfloat16

D_MODEL = 2048
GRID_W = 64
EPS = 1e-6

NA_HEADS = 8
NA_HEAD_DIM = 128
NA_WIDTH = NA_HEADS * NA_HEAD_DIM
NA_KR = 8
NA_KC = 16
NA_ROWS_PER_STEP = 32

ML_HEADS = 4
ML_HEAD_DIM = 256
ML_WIDTH = ML_HEADS * ML_HEAD_DIM
ML_CHUNK = 256
ML_STRIP = 256
MIX_TN = 1024

SG_GROUPS = 8
SG_CHUNK = 128
SG_WIDTH = 1024
SG_GROUP_DIM = SG_WIDTH // SG_GROUPS

D_FF = 4 * D_MODEL
N_BRANCH = 3

OFF_A = 0
OFF_B = OFF_A + 3 * NA_WIDTH
OFF_BG = OFF_B + 4 * ML_WIDTH
OFF_C = OFF_BG + 4 * ML_HEADS
OFF_G = OFF_C + 2 * SG_WIDTH
D_IN = OFF_G + N_BRANCH * D_MODEL
N_GATE = 4 * ML_HEADS

P_A = 0
P_B = P_A + 3 * NA_WIDTH
P_C = P_B + 4 * ML_WIDTH
P_G = P_C + 2 * SG_WIDTH
P_WIDTH = P_G + N_BRANCH * D_MODEL

LANE = 128
HALO_ROWS = 16
VMEM_LIMIT = 56 * 1024 * 1024
BIG_VMEM_LIMIT = 60 * 1024 * 1024
LOG2E = np.float32(1.4426950408889634)


def _cparams(sem, vmem=VMEM_LIMIT, **kw):
    return pltpu.CompilerParams(dimension_semantics=sem, vmem_limit_bytes=vmem, **kw)


def _rms(x, g):
    return x * lax.rsqrt(jnp.mean(x * x, axis=-1, keepdims=True) + EPS) * g


def _sigmoid(x):
    return 1.0 / (1.0 + jnp.exp(-x))


def _log_sigmoid(x):
    return jnp.minimum(x, 0.0) - jnp.log(1.0 + jnp.exp(-jnp.abs(x)))


def _gelu_tanh(x):
    k = -2.0 * np.sqrt(2.0 / np.pi) * float(LOG2E)
    z = x * (np.float32(k * 0.044715) * (x * x) + np.float32(k))
    return x * (1.0 / (1.0 + jnp.exp2(z)))


def _dot(a, b):
    return jnp.dot(a, b, preferred_element_type=F32)


def _dot_nt(a, b):
    return lax.dot_general(a, b, (((1,), (1,)), ((), ())), preferred_element_type=F32)


def _dot_tn(a, b):
    return lax.dot_general(a, b, (((0,), (0,)), ((), ())), preferred_element_type=F32)


def _inproj_kernel(x_ref, g_ref, w_ref, wg_ref, wgt_ref, bgc_ref, bgr_ref,
                   o_ref, gc_ref, gr_ref, h_scr):
    @pl.when(pl.program_id(1) == 0)
    def _():
        hb = _rms(x_ref[...], g_ref[...]).astype(BF16)
        h_scr[...] = hb
        gc_ref[...] = _dot(hb, wg_ref[...]) + bgc_ref[...]
        gr_ref[...] = _dot_nt(wgt_ref[...], hb) + bgr_ref[...]

    o_ref[...] = _dot(h_scr[...], w_ref[...]).astype(BF16)


def _in_proj(xf, g, w_main, l, w_gate, w_gate_t, bg_col, bg_row, tm=1024, tn=2560):
    n = xf.shape[0]
    tm = min(tm, n)
    grid = (n // tm, P_WIDTH // tn)
    return pl.pallas_call(
        _inproj_kernel,
        grid=grid,
        in_specs=[
            pl.BlockSpec((tm, D_MODEL), lambda i, j: (i, 0)),
            pl.BlockSpec((1, D_MODEL), lambda i, j: (0, 0)),
            pl.BlockSpec((None, D_MODEL, tn), lambda i, j: (l, 0, j)),
            pl.BlockSpec((D_MODEL, LANE), lambda i, j: (0, 0)),
            pl.BlockSpec((N_GATE, D_MODEL), lambda i, j: (0, 0)),
            pl.BlockSpec((1, LANE), lambda i, j: (0, 0)),
            pl.BlockSpec((N_GATE, 1), lambda i, j: (0, 0)),
        ],
        out_specs=[
            pl.BlockSpec((tm, tn), lambda i, j: (i, j)),
            pl.BlockSpec((tm, LANE), lambda i, j: (i, 0)),
            pl.BlockSpec((N_GATE, tm), lambda i, j: (0, i)),
        ],
        out_shape=[
            jax.ShapeDtypeStruct((n, P_WIDTH), BF16),
            jax.ShapeDtypeStruct((n, LANE), F32),
            jax.ShapeDtypeStruct((N_GATE, n), F32),
        ],
        scratch_shapes=[pltpu.VMEM((tm, D_MODEL), BF16)],
        compiler_params=_cparams(("parallel", "arbitrary"), vmem=BIG_VMEM_LIMIT),
        name="in_proj",
    )(xf, g, w_main, w_gate, w_gate_t, bg_col, bg_row)


def _na_bias_table(rpb):
    qc = np.arange(GRID_W)[:, None]
    kc = np.arange(GRID_W)[None, :]
    cs = np.clip(qc - NA_KC // 2, 0, GRID_W - NA_KC)
    ok = (kc >= cs) & (kc < cs + NA_KC)
    dc = np.clip(kc - qc + NA_KC - 1, 0, 2 * NA_KC - 2)
    onehot = (dc[None] == np.arange(2 * NA_KC - 1)[:, None, None]).astype(np.float32)
    col = jnp.einsum("hrc,cqk->hrqk", rpb.astype(F32), onehot, precision=lax.Precision.HIGHEST)
    col = jnp.where(jnp.asarray(ok)[None, None], col * LOG2E, -1e30)
    tab = jnp.stack([col[:, d:d + NA_KR] for d in range(NA_KR)], axis=1)
    tab = tab.transpose(0, 1, 3, 2, 4)
    return tab.reshape(NA_HEADS, NA_KR, GRID_W, NA_KR * GRID_W)


def _na_kernel(q_ref, k_ref, v_ref, bias_ref, o_ref, *, rows, rps):
    scale = np.float32(NA_HEAD_DIM ** -0.5 * float(LOG2E))
    win = NA_KR * GRID_W

    def body(i, carry):
        rr = [i * rps + u for u in range(rps)]
        rs = [jnp.clip(r - NA_KR // 2, 0, rows - NA_KR) for r in rr]
        q0 = [pl.multiple_of(r * GRID_W, GRID_W) for r in rr]
        k0 = [pl.multiple_of(x * GRID_W, GRID_W) for x in rs]
        s = [_dot_nt(q_ref[pl.ds(q0[u], GRID_W), :], k_ref[pl.ds(k0[u], win), :]) for u in range(rps)]
        p, inv = [], []
        for u in range(rps):
            su = s[u] * scale + bias_ref[rs[u] - rr[u] + NA_KR - 1]
            e = jnp.exp2(su - jnp.max(su, axis=-1, keepdims=True))
            inv.append(1.0 / jnp.sum(e, axis=-1, keepdims=True))
            p.append(e.astype(BF16))
        for u in range(rps):
            o = _dot(p[u], v_ref[pl.ds(k0[u], win), :]) * inv[u]
            o_ref[pl.ds(q0[u], GRID_W), :] = o.astype(BF16)
        return carry

    lax.fori_loop(0, rows // rps, body, 0)


def _na(proj3, bias_tab):
    b, t, _ = proj3.shape
    rows = t // GRID_W
    blk = lambda off: pl.BlockSpec((None, t, NA_HEAD_DIM), lambda bi, h: (bi, 0, off + h))
    return pl.pallas_call(
        functools.partial(_na_kernel, rows=rows, rps=math.gcd(rows, NA_ROWS_PER_STEP)),
        grid=(b, NA_HEADS),
        in_specs=[
            blk(P_A // NA_HEAD_DIM),
            blk(P_A // NA_HEAD_DIM + NA_HEADS),
            blk(P_A // NA_HEAD_DIM + 2 * NA_HEADS),
            pl.BlockSpec((None, NA_KR, GRID_W, NA_KR * GRID_W), lambda bi, h: (h, 0, 0, 0)),
        ],
        out_specs=pl.BlockSpec((None, t, NA_HEAD_DIM), lambda bi, h: (bi, 0, h)),
        out_shape=jax.ShapeDtypeStruct((b, t, NA_WIDTH), BF16),
        compiler_params=_cparams(("parallel", "parallel")),
        name="na",
    )(proj3, proj3, proj3, bias_tab)


def _tri_product(tri_b, x, *, left):
    hi = x.astype(BF16)
    rest = x - hi.astype(F32)
    mid = rest.astype(BF16)
    lo = (rest - mid.astype(F32)).astype(BF16)
    if left:
        return (_dot(tri_b, lo) + _dot(tri_b, mid)) + _dot(tri_b, hi)
    return (_dot(lo, tri_b) + _dot(mid, tri_b)) + _dot(hi, tri_b)


def _mlstm_init(c_scr, n_scr, m_scr):
    @pl.when(pl.program_id(1) == 0)
    def _():
        c_scr[...] = jnp.zeros_like(c_scr)
        n_scr[...] = jnp.zeros_like(n_scr)
        m_scr[...] = jnp.zeros_like(m_scr)


def _mlstm_chunk(q_of, k_of, v_ref, gc_ref, gr_ref, c_scr, n_scr, m_scr, emit, *, reverse, L):
    ri = lax.broadcasted_iota(jnp.int32, (L, L), 0)
    ci = lax.broadcasted_iota(jnp.int32, (L, L), 1)
    mask = (ci >= ri) if reverse else (ci <= ri)
    mask_t = (ri >= ci) if reverse else (ri <= ci)
    gi = 2 if reverse else 0
    last = 0 if reverse else L - 1
    R = ML_STRIP

    gcol = gc_ref[...]
    grow = gr_ref[...]
    b_col_all = _tri_product(jnp.where(mask, 1.0, 0.0).astype(BF16), _log_sigmoid(gcol), left=True)
    b_row_all = _tri_product(jnp.where(mask_t, 1.0, 0.0).astype(BF16), _log_sigmoid(grow), left=False)
    b2_col_all = b_col_all * LOG2E

    heads = range(ML_HEADS)
    hsl = [slice(hd * ML_HEAD_DIM, (hd + 1) * ML_HEAD_DIM) for hd in heads]
    ii = [gi * ML_HEADS + hd for hd in heads]
    fi = [(gi + 1) * ML_HEADS + hd for hd in heads]
    m_prev = [m_scr[hd][:, 0:1] for hd in heads]
    c_prev = [c_scr[hd] for hd in heads]
    n_prev = [n_scr[hd] for hd in heads]
    q = [q_of(hd) for hd in heads]
    k = [k_of(hd) for hd in heads]
    vb = [v_ref[:, hsl[hd]] for hd in heads]

    strips = []
    for st in range(L // R):
        if reverse:
            c0, c1 = (st * R) // LANE * LANE, L
        else:
            c0, c1 = 0, -(-((st + 1) * R) // LANE) * LANE
        strips.append((st, slice(st * R, (st + 1) * R), slice(c0, c1)))
    units = [(hd, st, rows, cols) for hd in heads for (st, rows, cols) in strips]

    qk = [_dot_nt(q[hd][1][rows], k[hd][1][cols]) for (hd, st, rows, cols) in units]
    qc = [_dot(q[hd][1][rows], c_prev[hd].astype(BF16)) for (hd, st, rows, cols) in units]

    sb, scale_c, inv = [], [], []
    for u, (hd, st, rows, cols) in enumerate(units):
        c0, c1 = cols.start, cols.stop
        b2_col = b2_col_all[rows, fi[hd]:fi[hd] + 1]
        r2_row = (grow[ii[hd]:ii[hd] + 1, cols] - b_row_all[fi[hd]:fi[hd] + 1, cols]) * LOG2E
        sri = lax.broadcasted_iota(jnp.int32, (R, c1 - c0), 0) + st * R
        sci = lax.broadcasted_iota(jnp.int32, (R, c1 - c0), 1) + c0
        smask = (sci >= sri) if reverse else (sci <= sri)
        d2 = jnp.where(smask, b2_col + r2_row, -1e30)
        inter2 = b2_col + m_prev[hd] * LOG2E
        m2 = jnp.maximum(inter2, jnp.max(d2, axis=1, keepdims=True))
        s = qk[u] * jnp.exp2(d2 - m2)
        w_inter = jnp.exp2(inter2 - m2)
        nq = (jnp.sum(s, axis=1, keepdims=True)
              + w_inter * jnp.sum(q[hd][0][rows] * n_prev[hd], axis=1, keepdims=True))
        r = 1.0 / jnp.maximum(jnp.abs(nq), jnp.exp2(-m2))
        sb.append(s.astype(BF16))
        scale_c.append(w_inter * r)
        inv.append(r)

    for u, (hd, st, rows, cols) in enumerate(units):
        emit(hd, rows, _dot(sb[u], vb[hd][cols]) * inv[u] + qc[u] * scale_c[u])

    kws, decays, m_news = [], [], []
    for hd in heads:
        b_col = b_col_all[:, fi[hd]:fi[hd] + 1]
        b_last = b_col_all[last:last + 1, fi[hd]:fi[hd] + 1]
        wlog = b_last - b_col + gcol[:, ii[hd]:ii[hd] + 1]
        m_new = jnp.maximum(b_last + m_prev[hd], jnp.max(wlog, axis=0, keepdims=True))
        kws.append(k[hd][0] * jnp.exp(wlog - m_new))
        decays.append(jnp.exp(b_last + m_prev[hd] - m_new))
        m_news.append(m_new)
    eye = jnp.where(lax.broadcasted_iota(jnp.int32, (ML_HEAD_DIM, ML_HEAD_DIM), 0)
                    == lax.broadcasted_iota(jnp.int32, (ML_HEAD_DIM, ML_HEAD_DIM), 1), 1.0, 0.0).astype(BF16)
    kwt = [_dot_nt(eye, kws[hd].astype(BF16)).astype(BF16) for hd in heads]
    for hd in heads:
        c_scr[hd] = decays[hd] * c_prev[hd] + _dot(kwt[hd], vb[hd])
        n_scr[hd] = decays[hd] * n_prev[hd] + jnp.sum(kws[hd], axis=0, keepdims=True)
        m_scr[hd] = jnp.broadcast_to(m_news[hd], (1, LANE))


def _mlstm_fwd_kernel(qp_ref, q_ref, qn_ref, kp_ref, k_ref, kn_ref, v_ref, gc_ref, gr_ref, cw_ref,
                      h_ref, qs_ref, ks_ref, c_scr, n_scr, m_scr, *, nc, L):
    c = pl.program_id(1)
    _mlstm_init(c_scr, n_scr, m_scr)

    rowi = lax.broadcasted_iota(jnp.int32, (L, 1), 0)
    has_prev = (c > 0).astype(F32)
    has_next = (c < nc - 1).astype(F32)

    def conv_silu(p_ref, x_ref, n_ref, w):
        x = x_ref[...].astype(F32)
        prev_row = p_ref[HALO_ROWS - 1:HALO_ROWS, :].astype(F32) * has_prev
        next_row = n_ref[0:1, :].astype(F32) * has_next
        xp = jnp.where(rowi == 0, prev_row, pltpu.roll(x, 1, 0))
        xn = jnp.where(rowi == L - 1, next_row, pltpu.roll(x, L - 1, 0))
        y = xp * w[0:1] + x * w[1:2] + xn * w[2:3]
        return y * _sigmoid(y)

    cw = cw_ref[...]
    qs = conv_silu(qp_ref, q_ref, qn_ref, cw[:, :ML_WIDTH]) * np.float32(ML_HEAD_DIM ** -0.5)
    ks = conv_silu(kp_ref, k_ref, kn_ref, cw[:, ML_WIDTH:])
    qsb = qs.astype(BF16)
    ksb = ks.astype(BF16)
    qs_ref[...] = qsb
    ks_ref[...] = ksb

    def head(x, xb):
        return lambda hd: (x[:, hd * ML_HEAD_DIM:(hd + 1) * ML_HEAD_DIM],
                           xb[:, hd * ML_HEAD_DIM:(hd + 1) * ML_HEAD_DIM])

    def emit(hd, rows, h):
        h_ref[rows, hd * ML_HEAD_DIM:(hd + 1) * ML_HEAD_DIM] = h

    _mlstm_chunk(head(qs, qsb), head(ks, ksb), v_ref, gc_ref, gr_ref, c_scr, n_scr, m_scr, emit,
                 reverse=False, L=L)


def _mlstm_bwd_kernel(qs_ref, ks_ref, v_ref, gc_ref, gr_ref, hf_ref, o_ref, g_ref,
                      y_ref, c_scr, n_scr, m_scr, *, L):
    _mlstm_init(c_scr, n_scr, m_scr)

    def head(x_ref):
        def get(hd):
            xb = x_ref[:, hd * ML_HEAD_DIM:(hd + 1) * ML_HEAD_DIM]
            return xb.astype(F32), xb
        return get

    def emit(hd, rows, h):
        sl = slice(hd * ML_HEAD_DIM, (hd + 1) * ML_HEAD_DIM)
        y = _rms(h + hf_ref[rows, sl], g_ref[:, sl]) * _sigmoid(o_ref[rows, sl].astype(F32))
        y_ref[rows, sl] = y.astype(BF16)

    _mlstm_chunk(head(qs_ref), head(ks_ref), v_ref, gc_ref, gr_ref, c_scr, n_scr, m_scr, emit,
                 reverse=True, L=L)


def _mlstm_scratch():
    return [pltpu.VMEM((ML_HEADS, ML_HEAD_DIM, ML_HEAD_DIM), F32),
            pltpu.VMEM((ML_HEADS, 1, ML_HEAD_DIM), F32),
            pltpu.VMEM((ML_HEADS, 1, LANE), F32)]


def _mlstm_fwd(proj3, gate_col, gate_row, conv_w):
    b, t, _ = proj3.shape
    L = ML_CHUNK
    nc = t // L
    hb = L // HALO_ROWS
    n_halo = t // HALO_ROWS
    cur = lambda col: pl.BlockSpec((None, L, ML_WIDTH), lambda bi, c: (bi, c, col))
    prv = lambda col: pl.BlockSpec((None, HALO_ROWS, ML_WIDTH),
                                   lambda bi, c: (bi, jnp.maximum(c * hb - 1, 0), col))
    nxt = lambda col: pl.BlockSpec((None, HALO_ROWS, ML_WIDTH),
                                   lambda bi, c: (bi, jnp.minimum((c + 1) * hb, n_halo - 1), col))
    qc, kc, vc = P_B // ML_WIDTH, P_B // ML_WIDTH + 1, P_B // ML_WIDTH + 2
    return pl.pallas_call(
        functools.partial(_mlstm_fwd_kernel, nc=nc, L=L),
        grid=(b, nc),
        in_specs=[
            prv(qc), cur(qc), nxt(qc),
            prv(kc), cur(kc), nxt(kc),
            cur(vc),
            pl.BlockSpec((L, LANE), lambda bi, c: (bi * nc + c, 0)),
            pl.BlockSpec((N_GATE, L), lambda bi, c: (0, bi * nc + c)),
            pl.BlockSpec((3, 2 * ML_WIDTH), lambda bi, c: (0, 0)),
        ],
        out_specs=[cur(0), cur(0), cur(0)],
        out_shape=[jax.ShapeDtypeStruct((b, t, ML_WIDTH), F32),
                   jax.ShapeDtypeStruct((b, t, ML_WIDTH), BF16),
                   jax.ShapeDtypeStruct((b, t, ML_WIDTH), BF16)],
        scratch_shapes=_mlstm_scratch(),
        compiler_params=_cparams(("parallel", "arbitrary")),
        name="mlstm_fwd",
    )(proj3, proj3, proj3, proj3, proj3, proj3, proj3, gate_col, gate_row, conv_w)


def _mlstm_bwd(qs, ks, proj3, gate_col, gate_row, h_f, g):
    b, t, _ = proj3.shape
    L = ML_CHUNK
    nc = t // L
    cur = lambda col: pl.BlockSpec((None, L, ML_WIDTH), lambda bi, c: (bi, nc - 1 - c, col))
    vc, oc = P_B // ML_WIDTH + 2, P_B // ML_WIDTH + 3
    return pl.pallas_call(
        functools.partial(_mlstm_bwd_kernel, L=L),
        grid=(b, nc),
        in_specs=[
            cur(0), cur(0), cur(vc),
            pl.BlockSpec((L, LANE), lambda bi, c: (bi * nc + nc - 1 - c, 0)),
            pl.BlockSpec((N_GATE, L), lambda bi, c: (0, bi * nc + nc - 1 - c)),
            cur(0), cur(oc),
            pl.BlockSpec((1, ML_WIDTH), lambda bi, c: (0, 0)),
        ],
        out_specs=cur(0),
        out_shape=jax.ShapeDtypeStruct((b, t, ML_WIDTH), BF16),
        scratch_shapes=_mlstm_scratch(),
        compiler_params=_cparams(("parallel", "arbitrary")),
        name="mlstm_bwd",
    )(qs, ks, proj3, gate_col, gate_row, h_f, proj3, g)


def _sg_kernel(u_ref, v_ref, g_ref, ws_ref, bst_ref, y_ref, *, chunks):
    for cidx in range(chunks):
        rs = slice(cidx * SG_CHUNK, (cidx + 1) * SG_CHUNK)
        zu = _gelu_tanh(u_ref[rs, :].astype(F32))
        zv = _gelu_tanh(v_ref[rs, :].astype(F32))
        vb = _rms(zv, g_ref[...]).astype(BF16)
        for gidx in range(SG_GROUPS):
            cs = slice(gidx * SG_GROUP_DIM, (gidx + 1) * SG_GROUP_DIM)
            mixed = _dot(ws_ref[gidx], vb[:, cs]) + bst_ref[:, gidx:gidx + 1]
            y_ref[rs, cs] = (zu[:, cs] * mixed).astype(BF16)


def _sg(proj, g, ws, bst, chunks=4):
    n = proj.shape[0]
    tm = chunks * SG_CHUNK
    row = lambda col: pl.BlockSpec((tm, SG_WIDTH), lambda i: (i, col))
    return pl.pallas_call(
        functools.partial(_sg_kernel, chunks=chunks),
        grid=(n // tm,),
        in_specs=[row(P_C // SG_WIDTH), row(P_C // SG_WIDTH + 1),
                  pl.BlockSpec((1, SG_WIDTH), lambda i: (0, 0)),
                  pl.BlockSpec((SG_GROUPS, SG_CHUNK, SG_CHUNK), lambda i: (0, 0, 0)),
                  pl.BlockSpec((SG_CHUNK, SG_GROUPS), lambda i: (0, 0))],
        out_specs=row(0),
        out_shape=jax.ShapeDtypeStruct((n, SG_WIDTH), BF16),
        compiler_params=_cparams(("parallel",)),
        name="sg",
    )(proj, proj, g, ws, bst)


def _mix_kernel(ya_ref, yb_ref, yc_ref, ga_ref, gb_ref, gc_ref, wa_ref, wb_ref, wc_ref,
                wo_ref, x_ref, g_ref, o_ref, m_scr, *, nj, tn):
    j = pl.program_id(1)

    def branch(y_ref, w_ref, gate_ref):
        return _sigmoid(gate_ref[...].astype(F32)) * _dot(y_ref[...], w_ref[j])

    merged = branch(ya_ref, wa_ref, ga_ref) + branch(yb_ref, wb_ref, gb_ref) + branch(yc_ref, wc_ref, gc_ref)
    m_scr[j] = merged.astype(BF16)

    @pl.when(j == nj - 1)
    def _():
        acc = _dot(m_scr[0], wo_ref[0:tn, :])
        for jj in range(1, nj):
            acc = acc + _dot(m_scr[jj], wo_ref[jj * tn:(jj + 1) * tn, :])
        o_ref[...] = x_ref[...] + _rms(acc, g_ref[...])


def _mix_out(ya, yb, yc, proj, wa, wb, wc, wo, l, xf, g, tm=512):
    n = xf.shape[0]
    _, nj, _, tn = wa.shape
    yspec = pl.BlockSpec((tm, NA_WIDTH), lambda i, j: (i, 0))
    gate = lambda k: pl.BlockSpec((tm, tn), lambda i, j: (i, (P_G + k * D_MODEL) // tn + j))
    wspec = pl.BlockSpec((None, nj, NA_WIDTH, tn), lambda i, j: (l, 0, 0, 0), pipeline_mode=pl.Buffered(1))
    return pl.pallas_call(
        functools.partial(_mix_kernel, nj=nj, tn=tn),
        grid=(n // tm, nj),
        in_specs=[yspec, yspec, yspec, gate(0), gate(1), gate(2), wspec, wspec, wspec,
                  pl.BlockSpec((None, D_MODEL, D_MODEL), lambda i, j: (l, 0, 0), pipeline_mode=pl.Buffered(1)),
                  pl.BlockSpec((tm, D_MODEL), lambda i, j: (i, 0)),
                  pl.BlockSpec((1, D_MODEL), lambda i, j: (0, 0))],
        out_specs=pl.BlockSpec((tm, D_MODEL), lambda i, j: (i, 0)),
        out_shape=jax.ShapeDtypeStruct((n, D_MODEL), F32),
        scratch_shapes=[pltpu.VMEM((nj, tm, tn), BF16)],
        compiler_params=_cparams(("parallel", "arbitrary"), vmem=BIG_VMEM_LIMIT),
        name="mix_out",
    )(ya, yb, yc, proj, proj, proj, wa, wb, wc, wo, xf, g)


def _mlp_kernel(x_ref, g1_ref, wu_ref, wd_ref, g2_ref, o_ref, h_scr, *, nj, tf, sub):
    j = pl.program_id(1)

    @pl.when(j == 0)
    def _():
        h_scr[...] = _rms(x_ref[...], g1_ref[...]).astype(BF16)
        o_ref[...] = jnp.zeros_like(o_ref)

    acc = o_ref[...]
    for s in range(tf // sub):
        a = jnp.maximum(_dot(h_scr[...], wu_ref[:, s * sub:(s + 1) * sub]), 0.0)
        acc = acc + _dot((a * a).astype(BF16), wd_ref[s * sub:(s + 1) * sub, :])
    o_ref[...] = acc

    @pl.when(j == nj - 1)
    def _():
        o_ref[...] = x_ref[...] + _rms(o_ref[...], g2_ref[...])


def _mlp(xf, g1, wu, wd, l, g2, tm=512, tf=2048, sub=1024):
    n = xf.shape[0]
    nj = D_FF // tf
    return pl.pallas_call(
        functools.partial(_mlp_kernel, nj=nj, tf=tf, sub=sub),
        grid=(n // tm, nj),
        in_specs=[pl.BlockSpec((tm, D_MODEL), lambda i, j: (i, 0)),
                  pl.BlockSpec((1, D_MODEL), lambda i, j: (0, 0)),
                  pl.BlockSpec((None, D_MODEL, tf), lambda i, j: (l, 0, j)),
                  pl.BlockSpec((None, tf, D_MODEL), lambda i, j: (l, j, 0)),
                  pl.BlockSpec((1, D_MODEL), lambda i, j: (0, 0))],
        out_specs=pl.BlockSpec((tm, D_MODEL), lambda i, j: (i, 0)),
        out_shape=jax.ShapeDtypeStruct((n, D_MODEL), F32),
        scratch_shapes=[pltpu.VMEM((tm, D_MODEL), BF16)],
        compiler_params=_cparams(("parallel", "arbitrary"), vmem=BIG_VMEM_LIMIT),
        name="mlp",
    )(xf, g1, wu, wd, g2)


def _regroup_kernel(xt_ref, xg_ref, o_ref, gt_ref):
    o_ref[...] = xt_ref[0].T.astype(BF16)

    @pl.when(pl.program_id(1) == 0)
    def _():
        gt_ref[...] = xg_ref[0].astype(BF16)


def _regroup_cast(w_in_t, tc=512):
    depth = w_in_t.shape[0]

    def src_row(l, c):
        r = c * tc
        return (l, pl.multiple_of(jnp.where(r < OFF_BG, r, r + N_GATE), N_GATE), 0)

    return pl.pallas_call(
        _regroup_kernel,
        grid=(depth, P_WIDTH // tc),
        in_specs=[pl.BlockSpec((pl.Element(1), pl.Element(tc), pl.Element(D_MODEL)), src_row),
                  pl.BlockSpec((pl.Element(1), pl.Element(N_GATE), pl.Element(D_MODEL)),
                               lambda l, c: (l, OFF_BG, 0))],
        out_specs=[pl.BlockSpec((None, D_MODEL, tc), lambda l, c: (l, 0, c)),
                   pl.BlockSpec((None, N_GATE, D_MODEL), lambda l, c: (l, 0, 0))],
        out_shape=[jax.ShapeDtypeStruct((depth, D_MODEL, P_WIDTH), BF16),
                   jax.ShapeDtypeStruct((depth, N_GATE, D_MODEL), BF16)],
        compiler_params=_cparams(("parallel", "arbitrary")),
        name="regroup_cast",
    )(w_in_t, w_in_t)


def _col_tiles(w, tn=MIX_TN):
    d, k, n = w.shape
    return w.astype(BF16).reshape(d, k, n // tn, tn).transpose(0, 2, 1, 3)


def _prep_shared(w_in, w_a, w_b, w_c, w_out, w_up, w_down):
    w_main, w_gate_t = _regroup_cast(jnp.swapaxes(w_in, 1, 2))
    return dict(w_main=w_main, w_gate_t=w_gate_t,
                w_a=_col_tiles(w_a), w_b=_col_tiles(w_b), w_c=_col_tiles(w_c),
                w_out=w_out.astype(BF16), w_up=w_up.astype(BF16), w_down=w_down.astype(BF16))


def _prep_layer(l, shared, pre_mix_g, post_mix_g, pre_mlp_g, post_mlp_g, b_gate, conv_w, na_rpb,
                ml_norm_g, sg_norm_g, w_s, b_s):
    w_gt = shared["w_gate_t"][l]
    p = dict(shared)
    p.update(
        l=l,
        pre_mix_g=pre_mix_g[l].reshape(1, D_MODEL), post_mix_g=post_mix_g[l].reshape(1, D_MODEL),
        pre_mlp_g=pre_mlp_g[l].reshape(1, D_MODEL), post_mlp_g=post_mlp_g[l].reshape(1, D_MODEL),
        w_gate=jnp.pad(w_gt.T, ((0, 0), (0, LANE - N_GATE))), w_gate_t=w_gt,
        bg_col=jnp.pad(b_gate[l].astype(F32), (0, LANE - N_GATE)).reshape(1, LANE),
        bg_row=b_gate[l].astype(F32).reshape(N_GATE, 1),
        conv_w=conv_w[l].astype(F32), bias_tab=_na_bias_table(na_rpb[l]),
        ml_norm_g=ml_norm_g[l].reshape(1, ML_WIDTH), sg_norm_g=sg_norm_g[l].reshape(1, SG_WIDTH),
        w_s=w_s[l].astype(BF16), bst=b_s[l].astype(F32).T,
    )
    return p


def _trunk_layer(x, p):
    b, t, _ = x.shape
    n = b * t
    l = p["l"]
    xf = x.reshape(n, D_MODEL)
    proj, gate_col, gate_row = _in_proj(xf, p["pre_mix_g"], p["w_main"], l, p["w_gate"], p["w_gate_t"],
                                        p["bg_col"], p["bg_row"])
    proj3 = proj.reshape(b, t, P_WIDTH)
    y_a = _na(proj3, p["bias_tab"]).reshape(n, NA_WIDTH)
    h_f, qs, ks = _mlstm_fwd(proj3, gate_col, gate_row, p["conv_w"])
    y_b = _mlstm_bwd(qs, ks, proj3, gate_col, gate_row, h_f, p["ml_norm_g"]).reshape(n, ML_WIDTH)
    y_c = _sg(proj, p["sg_norm_g"], p["w_s"], p["bst"])
    x1 = _mix_out(y_a, y_b, y_c, proj, p["w_a"], p["w_b"], p["w_c"], p["w_out"], l, xf, p["post_mix_g"])
    x2 = _mlp(x1, p["pre_mlp_g"], p["w_up"], p["w_down"], l, p["post_mlp_g"])
    return x2.reshape(b, t, D_MODEL)


def kernel(x_prompt, x_sample, pre_mix_g, post_mix_g, pre_mlp_g, post_mlp_g, w_in, b_gate, conv_w,
           na_rpb, ml_norm_g, sg_norm_g, w_s, b_s, w_a, w_b, w_c, w_out, w_up, w_down):
    shared = _prep_shared(w_in, w_a, w_b, w_c, w_out, w_up, w_down)
    layers = [_prep_layer(l, shared, pre_mix_g, post_mix_g, pre_mlp_g, post_mlp_g, b_gate, conv_w,
                          na_rpb, ml_norm_g, sg_norm_g, w_s, b_s)
              for l in range(w_in.shape[0])]

    def run(x):
        for p in layers:
            x = _trunk_layer(x, p)
        return x

    return (run(x_prompt), run(x_sample))
```

```python
import functools
import math

import numpy as np
import jax
import jax.numpy as jnp
from jax import lax
from jax.experimental import pallas as pl
from jax.experimental.pallas import tpu as pltpu

F32 = jnp.float32
BF16 = jnp.bfloat16

D_MODEL = 2048
GRID_W = 64
EPS = 1e-6

NA_HEADS = 8
NA_HEAD_DIM = 128
NA_WIDTH = NA_HEADS * NA_HEAD_DIM
NA_KR = 8
NA_KC = 16
NA_ROWS_PER_STEP = 32

ML_HEADS = 4
ML_HEAD_DIM = 256
ML_WIDTH = ML_HEADS * ML_HEAD_DIM
ML_CHUNK = 256
ML_STRIP = 256
MIX_TN = 1024

SG_GROUPS = 8
SG_CHUNK = 128
SG_WIDTH = 1024
SG_GROUP_DIM = SG_WIDTH // SG_GROUPS

D_FF = 4 * D_MODEL
N_BRANCH = 3

OFF_A = 0
OFF_B = OFF_A + 3 * NA_WIDTH
OFF_BG = OFF_B + 4 * ML_WIDTH
OFF_C = OFF_BG + 4 * ML_HEADS
OFF_G = OFF_C + 2 * SG_WIDTH
D_IN = OFF_G + N_BRANCH * D_MODEL
N_GATE = 4 * ML_HEADS

P_A = 0
P_B = P_A + 3 * NA_WIDTH
P_C = P_B + 4 * ML_WIDTH
P_G = P_C + 2 * SG_WIDTH
P_WIDTH = P_G + N_BRANCH * D_MODEL

LANE = 128
HALO_ROWS = 16
VMEM_LIMIT = 56 * 1024 * 1024
BIG_VMEM_LIMIT = 60 * 1024 * 1024
LOG2E = np.float32(1.4426950408889634)


def _cparams(sem, vmem=VMEM_LIMIT, **kw):
    return pltpu.CompilerParams(dimension_semantics=sem, vmem_limit_bytes=vmem, **kw)


def _rms(x, g):
    return x * lax.rsqrt(jnp.mean(x * x, axis=-1, keepdims=True) + EPS) * g


def _sigmoid(x):
    return 1.0 / (1.0 + jnp.exp(-x))


def _log_sigmoid(x):
    return jnp.minimum(x, 0.0) - jnp.log(1.0 + jnp.exp(-jnp.abs(x)))


def _gelu_tanh(x):
    k = -2.0 * np.sqrt(2.0 / np.pi) * float(LOG2E)
    z = x * (np.float32(k * 0.044715) * (x * x) + np.float32(k))
    return x * (1.0 / (1.0 + jnp.exp2(z)))


def _dot(a, b):
    return jnp.dot(a, b, preferred_element_type=F32)


def _dot_nt(a, b):
    return lax.dot_general(a, b, (((1,), (1,)), ((), ())), preferred_element_type=F32)


def _dot_tn(a, b):
    return lax.dot_general(a, b, (((0,), (0,)), ((), ())), preferred_element_type=F32)


def _inproj_kernel(x_ref, g_ref, w_ref, wg_ref, bgc_ref, o_ref, gc_ref, gr_ref, h_scr, inv_scr):
    @pl.when(pl.program_id(1) == 0)
    def _():
        x = x_ref[...]
        hb = (x * g_ref[...]).astype(BF16)
        h_scr[...] = hb
        inv = lax.rsqrt(jnp.mean(x * x, axis=-1, keepdims=True) + EPS)
        inv_scr[...] = jnp.broadcast_to(inv, inv_scr.shape)
        gc = _dot(hb, wg_ref[...]) * inv + bgc_ref[...]
        gc_ref[...] = gc
        gr_ref[...] = gc.T[:N_GATE, :]

    inv_rep = jnp.tile(inv_scr[...], (1, o_ref.shape[1] // LANE))
    o_ref[...] = (_dot(h_scr[...], w_ref[...]) * inv_rep).astype(BF16)


def _in_proj(xf, g, w_main, l, w_gate, bg_col, tm=1024, tn=2560):
    n = xf.shape[0]
    tm = min(tm, n)
    grid = (n // tm, P_WIDTH // tn)
    return pl.pallas_call(
        _inproj_kernel,
        grid=grid,
        in_specs=[
            pl.BlockSpec((tm, D_MODEL), lambda i, j: (i, 0)),
            pl.BlockSpec((1, D_MODEL), lambda i, j: (0, 0)),
            pl.BlockSpec((None, D_MODEL, tn), lambda i, j: (l, 0, j)),
            pl.BlockSpec((D_MODEL, LANE), lambda i, j: (0, 0)),
            pl.BlockSpec((1, LANE), lambda i, j: (0, 0)),
        ],
        out_specs=[
            pl.BlockSpec((tm, tn), lambda i, j: (i, j)),
            pl.BlockSpec((tm, LANE), lambda i, j: (i, 0)),
            pl.BlockSpec((N_GATE, tm), lambda i, j: (0, i)),
        ],
        out_shape=[
            jax.ShapeDtypeStruct((n, P_WIDTH), BF16),
            jax.ShapeDtypeStruct((n, LANE), F32),
            jax.ShapeDtypeStruct((N_GATE, n), F32),
        ],
        scratch_shapes=[pltpu.VMEM((tm, D_MODEL), BF16), pltpu.VMEM((tm, LANE), F32)],
        compiler_params=_cparams(("parallel", "arbitrary"), vmem=BIG_VMEM_LIMIT),
        name="in_proj",
    )(xf, g, w_main, w_gate, bg_col)


def _na_bias_table(rpb):
    qc = np.arange(GRID_W)[:, None]
    kc = np.arange(GRID_W)[None, :]
    cs = np.clip(qc - NA_KC // 2, 0, GRID_W - NA_KC)
    ok = (kc >= cs) & (kc < cs + NA_KC)
    dc = np.clip(kc - qc + NA_KC - 1, 0, 2 * NA_KC - 2)
    onehot = (dc[None] == np.arange(2 * NA_KC - 1)[:, None, None]).astype(np.float32)
    col = jnp.einsum("hrc,cqk->hrqk", rpb.astype(F32), onehot, precision=lax.Precision.HIGHEST)
    col = jnp.where(jnp.asarray(ok)[None, None], col * LOG2E, -1e30)
    tab = jnp.stack([col[:, d:d + NA_KR] for d in range(NA_KR)], axis=1)
    tab = tab.transpose(0, 1, 3, 2, 4)
    return tab.reshape(NA_HEADS, NA_KR, GRID_W, NA_KR * GRID_W)


def _na_kernel(q_ref, k_ref, v_ref, bias_ref, o_ref, *, rows, rps):
    scale = np.float32(NA_HEAD_DIM ** -0.5 * float(LOG2E))
    win = NA_KR * GRID_W

    def body(i, carry):
        rr = [i * rps + u for u in range(rps)]
        rs = [jnp.clip(r - NA_KR // 2, 0, rows - NA_KR) for r in rr]
        q0 = [pl.multiple_of(r * GRID_W, GRID_W) for r in rr]
        k0 = [pl.multiple_of(x * GRID_W, GRID_W) for x in rs]
        s = [_dot_nt(q_ref[pl.ds(q0[u], GRID_W), :], k_ref[pl.ds(k0[u], win), :]) for u in range(rps)]
        p, inv = [], []
        for u in range(rps):
            su = s[u] * scale + bias_ref[rs[u] - rr[u] + NA_KR - 1]
            e = jnp.exp2(su - jnp.max(su, axis=-1, keepdims=True))
            inv.append(1.0 / jnp.sum(e, axis=-1, keepdims=True))
            p.append(e.astype(BF16))
        for u in range(rps):
            o = _dot(p[u], v_ref[pl.ds(k0[u], win), :]) * inv[u]
            o_ref[pl.ds(q0[u], GRID_W), :] = o.astype(BF16)
        return carry

    lax.fori_loop(0, rows // rps, body, 0)


def _na(proj3, bias_tab):
    b, t, _ = proj3.shape
    rows = t // GRID_W
    blk = lambda off: pl.BlockSpec((None, t, NA_HEAD_DIM), lambda bi, h: (bi, 0, off + h))
    return pl.pallas_call(
        functools.partial(_na_kernel, rows=rows, rps=math.gcd(rows, NA_ROWS_PER_STEP)),
        grid=(b, NA_HEADS),
        in_specs=[
            blk(P_A // NA_HEAD_DIM),
            blk(P_A // NA_HEAD_DIM + NA_HEADS),
            blk(P_A // NA_HEAD_DIM + 2 * NA_HEADS),
            pl.BlockSpec((None, NA_KR, GRID_W, NA_KR * GRID_W), lambda bi, h: (h, 0, 0, 0)),
        ],
        out_specs=pl.BlockSpec((None, t, NA_HEAD_DIM), lambda bi, h: (bi, 0, h)),
        out_shape=jax.ShapeDtypeStruct((b, t, NA_WIDTH), BF16),
        compiler_params=_cparams(("parallel", "parallel")),
        name="na",
    )(proj3, proj3, proj3, bias_tab)


def _tri_product(tri_b, x, *, left):
    hi = x.astype(BF16)
    rest = x - hi.astype(F32)
    mid = rest.astype(BF16)
    lo = (rest - mid.astype(F32)).astype(BF16)
    if left:
        return (_dot(tri_b, lo) + _dot(tri_b, mid)) + _dot(tri_b, hi)
    return (_dot(lo, tri_b) + _dot(mid, tri_b)) + _dot(hi, tri_b)


def _mlstm_init(c_scr, n_scr, m_scr):
    @pl.when(pl.program_id(1) == 0)
    def _():
        c_scr[...] = jnp.zeros_like(c_scr)
        n_scr[...] = jnp.zeros_like(n_scr)
        m_scr[...] = jnp.zeros_like(m_scr)


def _mlstm_chunk(q_of, k_of, v_ref, gc_ref, gr_ref, c_scr, n_scr, m_scr, emit, *, reverse, L):
    ri = lax.broadcasted_iota(jnp.int32, (L, L), 0)
    ci = lax.broadcasted_iota(jnp.int32, (L, L), 1)
    mask = (ci >= ri) if reverse else (ci <= ri)
    mask_t = (ri >= ci) if reverse else (ri <= ci)
    gi = 2 if reverse else 0
    last = 0 if reverse else L - 1
    R = ML_STRIP

    gcol = gc_ref[...]
    grow = gr_ref[...]
    b_col_all = _tri_product(jnp.where(mask, 1.0, 0.0).astype(BF16), _log_sigmoid(gcol), left=True)
    b_row_all = _tri_product(jnp.where(mask_t, 1.0, 0.0).astype(BF16), _log_sigmoid(grow), left=False)
    b2_col_all = b_col_all * LOG2E

    heads = range(ML_HEADS)
    hsl = [slice(hd * ML_HEAD_DIM, (hd + 1) * ML_HEAD_DIM) for hd in heads]
    ii = [gi * ML_HEADS + hd for hd in heads]
    fi = [(gi + 1) * ML_HEADS + hd for hd in heads]
    m_prev = [m_scr[hd][:, 0:1] for hd in heads]
    c_prev = [c_scr[hd] for hd in heads]
    n_prev = [n_scr[hd] for hd in heads]
    q = [q_of(hd) for hd in heads]
    k = [k_of(hd) for hd in heads]
    vb = [v_ref[:, hsl[hd]] for hd in heads]

    strips = []
    for st in range(L // R):
        if reverse:
            c0, c1 = (st * R) // LANE * LANE, L
        else:
            c0, c1 = 0, -(-((st + 1) * R) // LANE) * LANE
        strips.append((st, slice(st * R, (st + 1) * R), slice(c0, c1)))
    units = [(hd, st, rows, cols) for hd in heads for (st, rows, cols) in strips]

    qk = [_dot_nt(q[hd][1][rows], k[hd][1][cols]) for (hd, st, rows, cols) in units]
    qc = [_dot(q[hd][1][rows], c_prev[hd].astype(BF16)) for (hd, st, rows, cols) in units]

    sb, scale_c, inv = [], [], []
    for u, (hd, st, rows, cols) in enumerate(units):
        c0, c1 = cols.start, cols.stop
        b2_col = b2_col_all[rows, fi[hd]:fi[hd] + 1]
        r2_row = (grow[ii[hd]:ii[hd] + 1, cols] - b_row_all[fi[hd]:fi[hd] + 1, cols]) * LOG2E
        sri = lax.broadcasted_iota(jnp.int32, (R, c1 - c0), 0) + st * R
        sci = lax.broadcasted_iota(jnp.int32, (R, c1 - c0), 1) + c0
        smask = (sci >= sri) if reverse else (sci <= sri)
        d2 = jnp.where(smask, b2_col + r2_row, -1e30)
        inter2 = b2_col + m_prev[hd] * LOG2E
        m2 = jnp.maximum(inter2, jnp.max(d2, axis=1, keepdims=True))
        s = qk[u] * jnp.exp2(d2 - m2)
        w_inter = jnp.exp2(inter2 - m2)
        nq = (jnp.sum(s, axis=1, keepdims=True)
              + w_inter * jnp.sum(q[hd][0][rows] * n_prev[hd], axis=1, keepdims=True))
        r = 1.0 / jnp.maximum(jnp.abs(nq), jnp.exp2(-m2))
        sb.append(s.astype(BF16))
        scale_c.append(w_inter * r)
        inv.append(r)

    for u, (hd, st, rows, cols) in enumerate(units):
        emit(hd, rows, _dot(sb[u], vb[hd][cols]) * inv[u] + qc[u] * scale_c[u])

    kws, decays, m_news = [], [], []
    for hd in heads:
        b_col = b_col_all[:, fi[hd]:fi[hd] + 1]
        b_last = b_col_all[last:last + 1, fi[hd]:fi[hd] + 1]
        wlog = b_last - b_col + gcol[:, ii[hd]:ii[hd] + 1]
        m_new = jnp.maximum(b_last + m_prev[hd], jnp.max(wlog, axis=0, keepdims=True))
        kws.append(k[hd][0] * jnp.exp(wlog - m_new))
        decays.append(jnp.exp(b_last + m_prev[hd] - m_new))
        m_news.append(m_new)
    eye = jnp.where(lax.broadcasted_iota(jnp.int32, (ML_HEAD_DIM, ML_HEAD_DIM), 0)
                    == lax.broadcasted_iota(jnp.int32, (ML_HEAD_DIM, ML_HEAD_DIM), 1), 1.0, 0.0).astype(BF16)
    kwt = [_dot_nt(eye, kws[hd].astype(BF16)).astype(BF16) for hd in heads]
    for hd in heads:
        c_scr[hd] = decays[hd] * c_prev[hd] + _dot(kwt[hd], vb[hd])
        n_scr[hd] = decays[hd] * n_prev[hd] + jnp.sum(kws[hd], axis=0, keepdims=True)
        m_scr[hd] = jnp.broadcast_to(m_news[hd], (1, LANE))


def _mlstm_fwd_kernel(qp_ref, q_ref, qn_ref, kp_ref, k_ref, kn_ref, v_ref, gc_ref, gr_ref, cw_ref,
                      h_ref, qs_ref, ks_ref, c_scr, n_scr, m_scr, *, nc, L):
    c = pl.program_id(1)
    _mlstm_init(c_scr, n_scr, m_scr)

    rowi = lax.broadcasted_iota(jnp.int32, (L, 1), 0)
    has_prev = (c > 0).astype(F32)
    has_next = (c < nc - 1).astype(F32)

    def conv_silu(p_ref, x_ref, n_ref, w):
        x = x_ref[...].astype(F32)
        prev_row = p_ref[HALO_ROWS - 1:HALO_ROWS, :].astype(F32) * has_prev
        next_row = n_ref[0:1, :].astype(F32) * has_next
        xp = jnp.where(rowi == 0, prev_row, pltpu.roll(x, 1, 0))
        xn = jnp.where(rowi == L - 1, next_row, pltpu.roll(x, L - 1, 0))
        y = xp * w[0:1] + x * w[1:2] + xn * w[2:3]
        return y * _sigmoid(y)

    cw = cw_ref[...]
    qs = conv_silu(qp_ref, q_ref, qn_ref, cw[:, :ML_WIDTH]) * np.float32(ML_HEAD_DIM ** -0.5)
    ks = conv_silu(kp_ref, k_ref, kn_ref, cw[:, ML_WIDTH:])
    qsb = qs.astype(BF16)
    ksb = ks.astype(BF16)
    qs_ref[...] = qsb
    ks_ref[...] = ksb

    def head(x, xb):
        return lambda hd: (x[:, hd * ML_HEAD_DIM:(hd + 1) * ML_HEAD_DIM],
                           xb[:, hd * ML_HEAD_DIM:(hd + 1) * ML_HEAD_DIM])

    def emit(hd, rows, h):
        h_ref[rows, hd * ML_HEAD_DIM:(hd + 1) * ML_HEAD_DIM] = h

    _mlstm_chunk(head(qs, qsb), head(ks, ksb), v_ref, gc_ref, gr_ref, c_scr, n_scr, m_scr, emit,
                 reverse=False, L=L)


def _mlstm_bwd_kernel(qs_ref, ks_ref, v_ref, gc_ref, gr_ref, hf_ref, o_ref, g_ref,
                      y_ref, c_scr, n_scr, m_scr, *, L):
    _mlstm_init(c_scr, n_scr, m_scr)

    def head(x_ref):
        def get(hd):
            xb = x_ref[:, hd * ML_HEAD_DIM:(hd + 1) * ML_HEAD_DIM]
            return xb.astype(F32), xb
        return get

    def emit(hd, rows, h):
        sl = slice(hd * ML_HEAD_DIM, (hd + 1) * ML_HEAD_DIM)
        y = _rms(h + hf_ref[rows, sl], g_ref[:, sl]) * _sigmoid(o_ref[rows, sl].astype(F32))
        y_ref[rows, sl] = y.astype(BF16)

    _mlstm_chunk(head(qs_ref), head(ks_ref), v_ref, gc_ref, gr_ref, c_scr, n_scr, m_scr, emit,
                 reverse=True, L=L)


def _mlstm_scratch():
    return [pltpu.VMEM((ML_HEADS, ML_HEAD_DIM, ML_HEAD_DIM), F32),
            pltpu.VMEM((ML_HEADS, 1, ML_HEAD_DIM), F32),
            pltpu.VMEM((ML_HEADS, 1, LANE), F32)]


def _mlstm_fwd(proj3, gate_col, gate_row, conv_w):
    b, t, _ = proj3.shape
    L = ML_CHUNK
    nc = t // L
    hb = L // HALO_ROWS
    n_halo = t // HALO_ROWS
    cur = lambda col: pl.BlockSpec((None, L, ML_WIDTH), lambda bi, c: (bi, c, col))
    prv = lambda col: pl.BlockSpec((None, HALO_ROWS, ML_WIDTH),
                                   lambda bi, c: (bi, jnp.maximum(c * hb - 1, 0), col))
    nxt = lambda col: pl.BlockSpec((None, HALO_ROWS, ML_WIDTH),
                                   lambda bi, c: (bi, jnp.minimum((c + 1) * hb, n_halo - 1), col))
    qc, kc, vc = P_B // ML_WIDTH, P_B // ML_WIDTH + 1, P_B // ML_WIDTH + 2
    return pl.pallas_call(
        functools.partial(_mlstm_fwd_kernel, nc=nc, L=L),
        grid=(b, nc),
        in_specs=[
            prv(qc), cur(qc), nxt(qc),
            prv(kc), cur(kc), nxt(kc),
            cur(vc),
            pl.BlockSpec((L, LANE), lambda bi, c: (bi * nc + c, 0)),
            pl.BlockSpec((N_GATE, L), lambda bi, c: (0, bi * nc + c)),
            pl.BlockSpec((3, 2 * ML_WIDTH), lambda bi, c: (0, 0)),
        ],
        out_specs=[cur(0), cur(0), cur(0)],
        out_shape=[jax.ShapeDtypeStruct((b, t, ML_WIDTH), F32),
                   jax.ShapeDtypeStruct((b, t, ML_WIDTH), BF16),
                   jax.ShapeDtypeStruct((b, t, ML_WIDTH), BF16)],
        scratch_shapes=_mlstm_scratch(),
        compiler_params=_cparams(("parallel", "arbitrary")),
        name="mlstm_fwd",
    )(proj3, proj3, proj3, proj3, proj3, proj3, proj3, gate_col, gate_row, conv_w)


def _mlstm_bwd(qs, ks, proj3, gate_col, gate_row, h_f, g):
    b, t, _ = proj3.shape
    L = ML_CHUNK
    nc = t // L
    cur = lambda col: pl.BlockSpec((None, L, ML_WIDTH), lambda bi, c: (bi, nc - 1 - c, col))
    vc, oc = P_B // ML_WIDTH + 2, P_B // ML_WIDTH + 3
    return pl.pallas_call(
        functools.partial(_mlstm_bwd_kernel, L=L),
        grid=(b, nc),
        in_specs=[
            cur(0), cur(0), cur(vc),
            pl.BlockSpec((L, LANE), lambda bi, c: (bi * nc + nc - 1 - c, 0)),
            pl.BlockSpec((N_GATE, L), lambda bi, c: (0, bi * nc + nc - 1 - c)),
            cur(0), cur(oc),
            pl.BlockSpec((1, ML_WIDTH), lambda bi, c: (0, 0)),
        ],
        out_specs=cur(0),
        out_shape=jax.ShapeDtypeStruct((b, t, ML_WIDTH), BF16),
        scratch_shapes=_mlstm_scratch(),
        compiler_params=_cparams(("parallel", "arbitrary")),
        name="mlstm_bwd",
    )(qs, ks, proj3, gate_col, gate_row, h_f, proj3, g)


def _sg_kernel(u_ref, v_ref, g_ref, ws_ref, bst_ref, y_ref, *, chunks):
    for cidx in range(chunks):
        rs = slice(cidx * SG_CHUNK, (cidx + 1) * SG_CHUNK)
        zu = _gelu_tanh(u_ref[rs, :].astype(F32))
        zv = _gelu_tanh(v_ref[rs, :].astype(F32))
        vb = _rms(zv, g_ref[...]).astype(BF16)
        for gidx in range(SG_GROUPS):
            cs = slice(gidx * SG_GROUP_DIM, (gidx + 1) * SG_GROUP_DIM)
            mixed = _dot(ws_ref[gidx], vb[:, cs]) + bst_ref[:, gidx:gidx + 1]
            y_ref[rs, cs] = (zu[:, cs] * mixed).astype(BF16)


def _sg(proj, g, ws, bst, chunks=4):
    n = proj.shape[0]
    tm = chunks * SG_CHUNK
    row = lambda col: pl.BlockSpec((tm, SG_WIDTH), lambda i: (i, col))
    return pl.pallas_call(
        functools.partial(_sg_kernel, chunks=chunks),
        grid=(n // tm,),
        in_specs=[row(P_C // SG_WIDTH), row(P_C // SG_WIDTH + 1),
                  pl.BlockSpec((1, SG_WIDTH), lambda i: (0, 0)),
                  pl.BlockSpec((SG_GROUPS, SG_CHUNK, SG_CHUNK), lambda i: (0, 0, 0)),
                  pl.BlockSpec((SG_CHUNK, SG_GROUPS), lambda i: (0, 0))],
        out_specs=row(0),
        out_shape=jax.ShapeDtypeStruct((n, SG_WIDTH), BF16),
        compiler_params=_cparams(("parallel",)),
        name="sg",
    )(proj, proj, g, ws, bst)


def _mix_kernel(ya_ref, yb_ref, yc_ref, ga_ref, gb_ref, gc_ref, wa_ref, wb_ref, wc_ref,
                wo_ref, x_ref, g_ref, o_ref, m_scr, *, nj, tn):
    j = pl.program_id(1)

    def branch(y_ref, w_ref, gate_ref):
        return _sigmoid(gate_ref[...].astype(F32)) * _dot(y_ref[...], w_ref[j])

    merged = branch(ya_ref, wa_ref, ga_ref) + branch(yb_ref, wb_ref, gb_ref) + branch(yc_ref, wc_ref, gc_ref)
    m_scr[j] = merged.astype(BF16)

    @pl.when(j == nj - 1)
    def _():
        acc = _dot(m_scr[0], wo_ref[0:tn, :])
        for jj in range(1, nj):
            acc = acc + _dot(m_scr[jj], wo_ref[jj * tn:(jj + 1) * tn, :])
        o_ref[...] = x_ref[...] + _rms(acc, g_ref[...])


def _mix_out(ya, yb, yc, proj, wa, wb, wc, wo, l, xf, g, tm=512):
    n = xf.shape[0]
    _, nj, _, tn = wa.shape
    yspec = pl.BlockSpec((tm, NA_WIDTH), lambda i, j: (i, 0))
    gate = lambda k: pl.BlockSpec((tm, tn), lambda i, j: (i, (P_G + k * D_MODEL) // tn + j))
    wspec = pl.BlockSpec((None, nj, NA_WIDTH, tn), lambda i, j: (l, 0, 0, 0), pipeline_mode=pl.Buffered(1))
    return pl.pallas_call(
        functools.partial(_mix_kernel, nj=nj, tn=tn),
        grid=(n // tm, nj),
        in_specs=[yspec, yspec, yspec, gate(0), gate(1), gate(2), wspec, wspec, wspec,
                  pl.BlockSpec((None, D_MODEL, D_MODEL), lambda i, j: (l, 0, 0), pipeline_mode=pl.Buffered(1)),
                  pl.BlockSpec((tm, D_MODEL), lambda i, j: (i, 0)),
                  pl.BlockSpec((1, D_MODEL), lambda i, j: (0, 0))],
        out_specs=pl.BlockSpec((tm, D_MODEL), lambda i, j: (i, 0)),
        out_shape=jax.ShapeDtypeStruct((n, D_MODEL), F32),
        scratch_shapes=[pltpu.VMEM((nj, tm, tn), BF16)],
        compiler_params=_cparams(("parallel", "arbitrary"), vmem=BIG_VMEM_LIMIT),
        name="mix_out",
    )(ya, yb, yc, proj, proj, proj, wa, wb, wc, wo, xf, g)


def _mlp_kernel(x_ref, g1_ref, wu_ref, wd_ref, g2_ref, o_ref, h_scr, *, nj, tf, sub):
    j = pl.program_id(1)

    @pl.when(j == 0)
    def _():
        h_scr[...] = _rms(x_ref[...], g1_ref[...]).astype(BF16)
        o_ref[...] = jnp.zeros_like(o_ref)

    acc = o_ref[...]
    for s in range(tf // sub):
        a = jnp.maximum(_dot(h_scr[...], wu_ref[:, s * sub:(s + 1) * sub]), 0.0)
        acc = acc + _dot((a * a).astype(BF16), wd_ref[s * sub:(s + 1) * sub, :])
    o_ref[...] = acc

    @pl.when(j == nj - 1)
    def _():
        o_ref[...] = x_ref[...] + _rms(o_ref[...], g2_ref[...])


def _mlp(xf, g1, wu, wd, l, g2, tm=512, tf=2048, sub=1024):
    n = xf.shape[0]
    nj = D_FF // tf
    return pl.pallas_call(
        functools.partial(_mlp_kernel, nj=nj, tf=tf, sub=sub),
        grid=(n // tm, nj),
        in_specs=[pl.BlockSpec((tm, D_MODEL), lambda i, j: (i, 0)),
                  pl.BlockSpec((1, D_MODEL), lambda i, j: (0, 0)),
                  pl.BlockSpec((None, D_MODEL, tf), lambda i, j: (l, 0, j)),
                  pl.BlockSpec((None, tf, D_MODEL), lambda i, j: (l, j, 0)),
                  pl.BlockSpec((1, D_MODEL), lambda i, j: (0, 0))],
        out_specs=pl.BlockSpec((tm, D_MODEL), lambda i, j: (i, 0)),
        out_shape=jax.ShapeDtypeStruct((n, D_MODEL), F32),
        scratch_shapes=[pltpu.VMEM((tm, D_MODEL), BF16)],
        compiler_params=_cparams(("parallel", "arbitrary"), vmem=BIG_VMEM_LIMIT),
        name="mlp",
    )(xf, g1, wu, wd, g2)


def _regroup_kernel(xt_ref, xg_ref, o_ref, gt_ref):
    o_ref[...] = xt_ref[0].T.astype(BF16)

    @pl.when(pl.program_id(1) == 0)
    def _():
        gt_ref[...] = xg_ref[0].astype(BF16)


def _regroup_cast(w_in_t, tc=512):
    depth = w_in_t.shape[0]

    def src_row(l, c):
        r = c * tc
        return (l, pl.multiple_of(jnp.where(r < OFF_BG, r, r + N_GATE), N_GATE), 0)

    return pl.pallas_call(
        _regroup_kernel,
        grid=(depth, P_WIDTH // tc),
        in_specs=[pl.BlockSpec((pl.Element(1), pl.Element(tc), pl.Element(D_MODEL)), src_row),
                  pl.BlockSpec((pl.Element(1), pl.Element(N_GATE), pl.Element(D_MODEL)),
                               lambda l, c: (l, OFF_BG, 0))],
        out_specs=[pl.BlockSpec((None, D_MODEL, tc), lambda l, c: (l, 0, c)),
                   pl.BlockSpec((None, N_GATE, D_MODEL), lambda l, c: (l, 0, 0))],
        out_shape=[jax.ShapeDtypeStruct((depth, D_MODEL, P_WIDTH), BF16),
                   jax.ShapeDtypeStruct((depth, N_GATE, D_MODEL), BF16)],
        compiler_params=_cparams(("parallel", "arbitrary")),
        name="regroup_cast",
    )(w_in_t, w_in_t)


def _col_tiles(w, tn=MIX_TN):
    d, k, n = w.shape
    return w.astype(BF16).reshape(d, k, n // tn, tn).transpose(0, 2, 1, 3)


def _prep_shared(w_in, w_a, w_b, w_c, w_out, w_up, w_down):
    w_main, w_gate_t = _regroup_cast(jnp.swapaxes(w_in, 1, 2))
    return dict(w_main=w_main, w_gate_t=w_gate_t,
                w_a=_col_tiles(w_a), w_b=_col_tiles(w_b), w_c=_col_tiles(w_c),
                w_out=w_out.astype(BF16), w_up=w_up.astype(BF16), w_down=w_down.astype(BF16))


def _prep_layer(l, shared, pre_mix_g, post_mix_g, pre_mlp_g, post_mlp_g, b_gate, conv_w, na_rpb,
                ml_norm_g, sg_norm_g, w_s, b_s):
    w_gt = shared["w_gate_t"][l]
    p = dict(shared)
    p.update(
        l=l,
        pre_mix_g=pre_mix_g[l].reshape(1, D_MODEL), post_mix_g=post_mix_g[l].reshape(1, D_MODEL),
        pre_mlp_g=pre_mlp_g[l].reshape(1, D_MODEL), post_mlp_g=post_mlp_g[l].reshape(1, D_MODEL),
        w_gate=jnp.pad(w_gt.T, ((0, 0), (0, LANE - N_GATE))),
        bg_col=jnp.pad(b_gate[l].astype(F32), (0, LANE - N_GATE)).reshape(1, LANE),
        conv_w=conv_w[l].astype(F32), bias_tab=_na_bias_table(na_rpb[l]),
        ml_norm_g=ml_norm_g[l].reshape(1, ML_WIDTH), sg_norm_g=sg_norm_g[l].reshape(1, SG_WIDTH),
        w_s=w_s[l].astype(BF16), bst=b_s[l].astype(F32).T,
    )
    return p


def _trunk_layer(x, p):
    b, t, _ = x.shape
    n = b * t
    l = p["l"]
    xf = x.reshape(n, D_MODEL)
    proj, gate_col, gate_row = _in_proj(xf, p["pre_mix_g"], p["w_main"], l, p["w_gate"], p["bg_col"])
    proj3 = proj.reshape(b, t, P_WIDTH)
    y_a = _na(proj3, p["bias_tab"]).reshape(n, NA_WIDTH)
    h_f, qs, ks = _mlstm_fwd(proj3, gate_col, gate_row, p["conv_w"])
    y_b = _mlstm_bwd(qs, ks, proj3, gate_col, gate_row, h_f, p["ml_norm_g"]).reshape(n, ML_WIDTH)
    y_c = _sg(proj, p["sg_norm_g"], p["w_s"], p["bst"])
    x1 = _mix_out(y_a, y_b, y_c, proj, p["w_a"], p["w_b"], p["w_c"], p["w_out"], l, xf, p["post_mix_g"])
    x2 = _mlp(x1, p["pre_mlp_g"], p["w_up"], p["w_down"], l, p["post_mlp_g"])
    return x2.reshape(b, t, D_MODEL)


def kernel(x_prompt, x_sample, pre_mix_g, post_mix_g, pre_mlp_g, post_mlp_g, w_in, b_gate, conv_w,
           na_rpb, ml_norm_g, sg_norm_g, w_s, b_s, w_a, w_b, w_c, w_out, w_up, w_down):
    shared = _prep_shared(w_in, w_a, w_b, w_c, w_out, w_up, w_down)
    layers = [_prep_layer(l, shared, pre_mix_g, post_mix_g, pre_mlp_g, post_mlp_g, b_gate, conv_w,
                          na_rpb, ml_norm_g, sg_norm_g, w_s, b_s)
              for l in range(w_in.shape[0])]

    def run(x):
        for p in layers:
            x = _trunk_layer(x, p)
        return x

    return (run(x_prompt), run(x_sample))
```

```python
import functools
import math

import numpy as np
import jax
import jax.numpy as jnp
from jax import lax
from jax.experimental import pallas as pl
from jax.experimental.pallas import tpu as pltpu

F32 = jnp.float32
BF16 = jnp.bfloat16

D_MODEL = 2048
GRID_W = 64
EPS = 1e-6

NA_HEADS = 8
NA_HEAD_DIM = 128
NA_WIDTH = NA_HEADS * NA_HEAD_DIM
NA_KR = 8
NA_KC = 16
NA_ROWS_PER_STEP = 64

ML_HEADS = 4
ML_HEAD_DIM = 256
ML_WIDTH = ML_HEADS * ML_HEAD_DIM
ML_CHUNK = 256
ML_STRIP = 256
MIX_TN = 1024

SG_GROUPS = 8
SG_CHUNK = 128
SG_WIDTH = 1024
SG_GROUP_DIM = SG_WIDTH // SG_GROUPS

D_FF = 4 * D_MODEL
N_BRANCH = 3

OFF_A = 0
OFF_B = OFF_A + 3 * NA_WIDTH
OFF_BG = OFF_B + 4 * ML_WIDTH
OFF_C = OFF_BG + 4 * ML_HEADS
OFF_G = OFF_C + 2 * SG_WIDTH
D_IN = OFF_G + N_BRANCH * D_MODEL
N_GATE = 4 * ML_HEADS

P_A = 0
P_B = P_A + 3 * NA_WIDTH
P_C = P_B + 4 * ML_WIDTH
P_G = P_C + 2 * SG_WIDTH
P_WIDTH = P_G + N_BRANCH * D_MODEL

LANE = 128
HALO_ROWS = 16
VMEM_LIMIT = 56 * 1024 * 1024
BIG_VMEM_LIMIT = 60 * 1024 * 1024
LOG2E = np.float32(1.4426950408889634)


def _cparams(sem, vmem=VMEM_LIMIT, **kw):
    return pltpu.CompilerParams(dimension_semantics=sem, vmem_limit_bytes=vmem, **kw)


def _rms(x, g):
    return x * lax.rsqrt(jnp.mean(x * x, axis=-1, keepdims=True) + EPS) * g


def _sigmoid(x):
    return 1.0 / (1.0 + jnp.exp(-x))


def _log_sigmoid(x):
    return jnp.minimum(x, 0.0) - jnp.log(1.0 + jnp.exp(-jnp.abs(x)))


def _gelu_tanh(x):
    k = -2.0 * np.sqrt(2.0 / np.pi) * float(LOG2E)
    z = x * (np.float32(k * 0.044715) * (x * x) + np.float32(k))
    return x * (1.0 / (1.0 + jnp.exp2(z)))


def _dot(a, b):
    return jnp.dot(a, b, preferred_element_type=F32)


def _dot_nt(a, b):
    return lax.dot_general(a, b, (((1,), (1,)), ((), ())), preferred_element_type=F32)


def _inproj_kernel(x_ref, g_ref, w_ref, wg_ref, bgc_ref, o_ref, gc_ref, gr_ref, h_scr, inv_scr):
    @pl.when(pl.program_id(1) == 0)
    def _():
        x = x_ref[...]
        hb = (x * g_ref[...]).astype(BF16)
        h_scr[...] = hb
        inv = lax.rsqrt(jnp.mean(x * x, axis=-1, keepdims=True) + EPS)
        inv_scr[...] = jnp.broadcast_to(inv, inv_scr.shape)
        gc = _dot(hb, wg_ref[...]) * inv + bgc_ref[...]
        gc_ref[...] = gc
        gr_ref[...] = gc.T[:N_GATE, :]

    inv_rep = jnp.tile(inv_scr[...], (1, o_ref.shape[1] // LANE))
    o_ref[...] = (_dot(h_scr[...], w_ref[...]) * inv_rep).astype(BF16)


def _in_proj(xf, g, w_main, l, w_gate, bg_col, tm=1024, tn=2560):
    n = xf.shape[0]
    tm = min(tm, n)
    grid = (n // tm, P_WIDTH // tn)
    return pl.pallas_call(
        _inproj_kernel,
        grid=grid,
        in_specs=[
            pl.BlockSpec((tm, D_MODEL), lambda i, j: (i, 0)),
            pl.BlockSpec((1, D_MODEL), lambda i, j: (0, 0)),
            pl.BlockSpec((None, D_MODEL, tn), lambda i, j: (l, 0, j)),
            pl.BlockSpec((D_MODEL, LANE), lambda i, j: (0, 0)),
            pl.BlockSpec((1, LANE), lambda i, j: (0, 0)),
        ],
        out_specs=[
            pl.BlockSpec((tm, tn), lambda i, j: (i, j)),
            pl.BlockSpec((tm, LANE), lambda i, j: (i, 0)),
            pl.BlockSpec((N_GATE, tm), lambda i, j: (0, i)),
        ],
        out_shape=[
            jax.ShapeDtypeStruct((n, P_WIDTH), BF16),
            jax.ShapeDtypeStruct((n, LANE), F32),
            jax.ShapeDtypeStruct((N_GATE, n), F32),
        ],
        scratch_shapes=[pltpu.VMEM((tm, D_MODEL), BF16), pltpu.VMEM((tm, LANE), F32)],
        compiler_params=_cparams(("parallel", "arbitrary"), vmem=BIG_VMEM_LIMIT),
        name="in_proj",
    )(xf, g, w_main, w_gate, bg_col)


def _na_bias_table(rpb):
    qc = np.arange(GRID_W)[:, None]
    kc = np.arange(GRID_W)[None, :]
    cs = np.clip(qc - NA_KC // 2, 0, GRID_W - NA_KC)
    ok = (kc >= cs) & (kc < cs + NA_KC)
    dc = np.clip(kc - qc + NA_KC - 1, 0, 2 * NA_KC - 2)
    onehot = (dc[None] == np.arange(2 * NA_KC - 1)[:, None, None]).astype(np.float32)
    col = jnp.einsum("hrc,cqk->hrqk", rpb.astype(F32), onehot, precision=lax.Precision.HIGHEST)
    col = jnp.where(jnp.asarray(ok)[None, None], col * LOG2E, -1e30)
    tab = jnp.stack([col[:, d:d + NA_KR] for d in range(NA_KR)], axis=1)
    tab = tab.transpose(0, 1, 3, 2, 4)
    return tab.reshape(NA_HEADS, NA_KR, GRID_W, NA_KR * GRID_W)


def _na_kernel(q_ref, k_ref, v_ref, bias_ref, o_ref, *, rows, rps):
    scale = np.float32(NA_HEAD_DIM ** -0.5 * float(LOG2E))
    win = NA_KR * GRID_W

    def body(i, carry):
        rr = [i * rps + u for u in range(rps)]
        rs = [jnp.clip(r - NA_KR // 2, 0, rows - NA_KR) for r in rr]
        q0 = [pl.multiple_of(r * GRID_W, GRID_W) for r in rr]
        k0 = [pl.multiple_of(x * GRID_W, GRID_W) for x in rs]
        s = [_dot_nt(q_ref[pl.ds(q0[u], GRID_W), :], k_ref[pl.ds(k0[u], win), :]) for u in range(rps)]
        p, inv = [], []
        for u in range(rps):
            su = s[u] * scale + bias_ref[rs[u] - rr[u] + NA_KR - 1]
            e = jnp.exp2(su - jnp.max(su, axis=-1, keepdims=True))
            inv.append(1.0 / jnp.sum(e, axis=-1, keepdims=True))
            p.append(e.astype(BF16))
        for u in range(rps):
            o = _dot(p[u], v_ref[pl.ds(k0[u], win), :]) * inv[u]
            o_ref[pl.ds(q0[u], GRID_W), :] = o.astype(BF16)
        return carry

    lax.fori_loop(0, rows // rps, body, 0)


def _na(proj3, bias_tab):
    b, t, _ = proj3.shape
    rows = t // GRID_W
    blk = lambda off: pl.BlockSpec((None, t, NA_HEAD_DIM), lambda bi, h: (bi, 0, off + h))
    return pl.pallas_call(
        functools.partial(_na_kernel, rows=rows, rps=math.gcd(rows, NA_ROWS_PER_STEP)),
        grid=(b, NA_HEADS),
        in_specs=[
            blk(P_A // NA_HEAD_DIM),
            blk(P_A // NA_HEAD_DIM + NA_HEADS),
            blk(P_A // NA_HEAD_DIM + 2 * NA_HEADS),
            pl.BlockSpec((None, NA_KR, GRID_W, NA_KR * GRID_W), lambda bi, h: (h, 0, 0, 0)),
        ],
        out_specs=pl.BlockSpec((None, t, NA_HEAD_DIM), lambda bi, h: (bi, 0, h)),
        out_shape=jax.ShapeDtypeStruct((b, t, NA_WIDTH), BF16),
        compiler_params=_cparams(("parallel", "parallel")),
        name="na",
    )(proj3, proj3, proj3, bias_tab)


def _tri_product(tri_b, x, *, left):
    hi = x.astype(BF16)
    rest = x - hi.astype(F32)
    mid = rest.astype(BF16)
    lo = (rest - mid.astype(F32)).astype(BF16)
    if left:
        return (_dot(tri_b, lo) + _dot(tri_b, mid)) + _dot(tri_b, hi)
    return (_dot(lo, tri_b) + _dot(mid, tri_b)) + _dot(hi, tri_b)


def _mlstm_init(c_scr, n_scr, m_scr):
    @pl.when(pl.program_id(1) == 0)
    def _():
        c_scr[...] = jnp.zeros_like(c_scr)
        n_scr[...] = jnp.zeros_like(n_scr)
        m_scr[...] = jnp.zeros_like(m_scr)


def _mlstm_chunk(q_of, k_of, v_ref, gc_ref, gr_ref, c_scr, n_scr, m_scr, emit, *, reverse, L):
    ri = lax.broadcasted_iota(jnp.int32, (L, L), 0)
    ci = lax.broadcasted_iota(jnp.int32, (L, L), 1)
    mask = (ci >= ri) if reverse else (ci <= ri)
    mask_t = (ri >= ci) if reverse else (ri <= ci)
    gi = 2 if reverse else 0
    last = 0 if reverse else L - 1
    R = ML_STRIP

    gcol = gc_ref[...]
    grow = gr_ref[...]
    b_col_all = _tri_product(jnp.where(mask, 1.0, 0.0).astype(BF16), _log_sigmoid(gcol), left=True)
    b_row_all = _tri_product(jnp.where(mask_t, 1.0, 0.0).astype(BF16), _log_sigmoid(grow), left=False)
    b2_col_all = b_col_all * LOG2E

    heads = range(ML_HEADS)
    hsl = [slice(hd * ML_HEAD_DIM, (hd + 1) * ML_HEAD_DIM) for hd in heads]
    ii = [gi * ML_HEADS + hd for hd in heads]
    fi = [(gi + 1) * ML_HEADS + hd for hd in heads]
    m_prev = [m_scr[hd][:, 0:1] for hd in heads]
    c_prev = [c_scr[hd] for hd in heads]
    n_prev = [n_scr[hd] for hd in heads]
    q = [q_of(hd) for hd in heads]
    k = [k_of(hd) for hd in heads]
    vb = [v_ref[:, hsl[hd]] for hd in heads]

    strips = []
    for st in range(L // R):
        if reverse:
            c0, c1 = (st * R) // LANE * LANE, L
        else:
            c0, c1 = 0, -(-((st + 1) * R) // LANE) * LANE
        strips.append((st, slice(st * R, (st + 1) * R), slice(c0, c1)))
    units = [(hd, st, rows, cols) for hd in heads for (st, rows, cols) in strips]

    qk = [_dot_nt(q[hd][1][rows], k[hd][1][cols]) for (hd, st, rows, cols) in units]
    qc = [_dot(q[hd][1][rows], c_prev[hd].astype(BF16)) for (hd, st, rows, cols) in units]

    sb, scale_c, inv = [], [], []
    for u, (hd, st, rows, cols) in enumerate(units):
        c0, c1 = cols.start, cols.stop
        b2_col = b2_col_all[rows, fi[hd]:fi[hd] + 1]
        r2_row = (grow[ii[hd]:ii[hd] + 1, cols] - b_row_all[fi[hd]:fi[hd] + 1, cols]) * LOG2E
        sri = lax.broadcasted_iota(jnp.int32, (R, c1 - c0), 0) + st * R
        sci = lax.broadcasted_iota(jnp.int32, (R, c1 - c0), 1) + c0
        smask = (sci >= sri) if reverse else (sci <= sri)
        d2 = jnp.where(smask, b2_col + r2_row, -1e30)
        inter2 = b2_col + m_prev[hd] * LOG2E
        m2 = jnp.maximum(inter2, jnp.max(d2, axis=1, keepdims=True))
        s = qk[u] * jnp.exp2(d2 - m2)
        w_inter = jnp.exp2(inter2 - m2)
        nq = (jnp.sum(s, axis=1, keepdims=True)
              + w_inter * jnp.sum(q[hd][0][rows] * n_prev[hd], axis=1, keepdims=True))
        r = 1.0 / jnp.maximum(jnp.abs(nq), jnp.exp2(-m2))
        sb.append(s.astype(BF16))
        scale_c.append(w_inter * r)
        inv.append(r)

    for u, (hd, st, rows, cols) in enumerate(units):
        emit(hd, rows, _dot(sb[u], vb[hd][cols]) * inv[u] + qc[u] * scale_c[u])

    kws, decays, m_news = [], [], []
    for hd in heads:
        b_col = b_col_all[:, fi[hd]:fi[hd] + 1]
        b_last = b_col_all[last:last + 1, fi[hd]:fi[hd] + 1]
        wlog = b_last - b_col + gcol[:, ii[hd]:ii[hd] + 1]
        m_new = jnp.maximum(b_last + m_prev[hd], jnp.max(wlog, axis=0, keepdims=True))
        kws.append(k[hd][0] * jnp.exp(wlog - m_new))
        decays.append(jnp.exp(b_last + m_prev[hd] - m_new))
        m_news.append(m_new)
    eye = jnp.where(lax.broadcasted_iota(jnp.int32, (ML_HEAD_DIM, ML_HEAD_DIM), 0)
                    == lax.broadcasted_iota(jnp.int32, (ML_HEAD_DIM, ML_HEAD_DIM), 1), 1.0, 0.0).astype(BF16)
    kwt = [_dot_nt(eye, kws[hd].astype(BF16)).astype(BF16) for hd in heads]
    for hd in heads:
        c_scr[hd] = decays[hd] * c_prev[hd] + _dot(kwt[hd], vb[hd])
        n_scr[hd] = decays[hd] * n_prev[hd] + jnp.sum(kws[hd], axis=0, keepdims=True)
        m_scr[hd] = jnp.broadcast_to(m_news[hd], (1, LANE))


def _mlstm_fwd_kernel(qp_ref, q_ref, qn_ref, kp_ref, k_ref, kn_ref, v_ref, gc_ref, gr_ref, cw_ref,
                      h_ref, qs_ref, ks_ref, c_scr, n_scr, m_scr, *, nc, L):
    c = pl.program_id(1)
    _mlstm_init(c_scr, n_scr, m_scr)

    rowi = lax.broadcasted_iota(jnp.int32, (L, 1), 0)
    has_prev = (c > 0).astype(F32)
    has_next = (c < nc - 1).astype(F32)

    def conv_silu(p_ref, x_ref, n_ref, w):
        x = x_ref[...].astype(F32)
        prev_row = p_ref[HALO_ROWS - 1:HALO_ROWS, :].astype(F32) * has_prev
        next_row = n_ref[0:1, :].astype(F32) * has_next
        xp = jnp.where(rowi == 0, prev_row, pltpu.roll(x, 1, 0))
        xn = jnp.where(rowi == L - 1, next_row, pltpu.roll(x, L - 1, 0))
        y = xp * w[0:1] + x * w[1:2] + xn * w[2:3]
        return y * _sigmoid(y)

    cw = cw_ref[...]
    qs = conv_silu(qp_ref, q_ref, qn_ref, cw[:, :ML_WIDTH]) * np.float32(ML_HEAD_DIM ** -0.5)
    ks = conv_silu(kp_ref, k_ref, kn_ref, cw[:, ML_WIDTH:])
    qsb = qs.astype(BF16)
    ksb = ks.astype(BF16)
    qs_ref[...] = qsb
    ks_ref[...] = ksb

    def head(x, xb):
        return lambda hd: (x[:, hd * ML_HEAD_DIM:(hd + 1) * ML_HEAD_DIM],
                           xb[:, hd * ML_HEAD_DIM:(hd + 1) * ML_HEAD_DIM])

    def emit(hd, rows, h):
        h_ref[rows, hd * ML_HEAD_DIM:(hd + 1) * ML_HEAD_DIM] = h

    _mlstm_chunk(head(qs, qsb), head(ks, ksb), v_ref, gc_ref, gr_ref, c_scr, n_scr, m_scr, emit,
                 reverse=False, L=L)


def _mlstm_bwd_kernel(qs_ref, ks_ref, v_ref, gc_ref, gr_ref, hf_ref, o_ref, g_ref,
                      y_ref, c_scr, n_scr, m_scr, *, L):
    _mlstm_init(c_scr, n_scr, m_scr)

    def head(x_ref):
        def get(hd):
            xb = x_ref[:, hd * ML_HEAD_DIM:(hd + 1) * ML_HEAD_DIM]
            return xb.astype(F32), xb
        return get

    def emit(hd, rows, h):
        sl = slice(hd * ML_HEAD_DIM, (hd + 1) * ML_HEAD_DIM)
        y = _rms(h + hf_ref[rows, sl], g_ref[:, sl]) * _sigmoid(o_ref[rows, sl].astype(F32))
        y_ref[rows, sl] = y.astype(BF16)

    _mlstm_chunk(head(qs_ref), head(ks_ref), v_ref, gc_ref, gr_ref, c_scr, n_scr, m_scr, emit,
                 reverse=True, L=L)


def _mlstm_scratch():
    return [pltpu.VMEM((ML_HEADS, ML_HEAD_DIM, ML_HEAD_DIM), F32),
            pltpu.VMEM((ML_HEADS, 1, ML_HEAD_DIM), F32),
            pltpu.VMEM((ML_HEADS, 1, LANE), F32)]


def _mlstm_fwd(proj3, gate_col, gate_row, conv_w):
    b, t, _ = proj3.shape
    L = ML_CHUNK
    nc = t // L
    hb = L // HALO_ROWS
    n_halo = t // HALO_ROWS
    cur = lambda col: pl.BlockSpec((None, L, ML_WIDTH), lambda bi, c: (bi, c, col))
    prv = lambda col: pl.BlockSpec((None, HALO_ROWS, ML_WIDTH),
                                   lambda bi, c: (bi, jnp.maximum(c * hb - 1, 0), col))
    nxt = lambda col: pl.BlockSpec((None, HALO_ROWS, ML_WIDTH),
                                   lambda bi, c: (bi, jnp.minimum((c + 1) * hb, n_halo - 1), col))
    qc, kc, vc = P_B // ML_WIDTH, P_B // ML_WIDTH + 1, P_B // ML_WIDTH + 2
    return pl.pallas_call(
        functools.partial(_mlstm_fwd_kernel, nc=nc, L=L),
        grid=(b, nc),
        in_specs=[
            prv(qc), cur(qc), nxt(qc),
            prv(kc), cur(kc), nxt(kc),
            cur(vc),
            pl.BlockSpec((L, LANE), lambda bi, c: (bi * nc + c, 0)),
            pl.BlockSpec((N_GATE, L), lambda bi, c: (0, bi * nc + c)),
            pl.BlockSpec((3, 2 * ML_WIDTH), lambda bi, c: (0, 0)),
        ],
        out_specs=[cur(0), cur(0), cur(0)],
        out_shape=[jax.ShapeDtypeStruct((b, t, ML_WIDTH), F32),
                   jax.ShapeDtypeStruct((b, t, ML_WIDTH), BF16),
                   jax.ShapeDtypeStruct((b, t, ML_WIDTH), BF16)],
        scratch_shapes=_mlstm_scratch(),
        compiler_params=_cparams(("parallel", "arbitrary")),
        name="mlstm_fwd",
    )(proj3, proj3, proj3, proj3, proj3, proj3, proj3, gate_col, gate_row, conv_w)


def _mlstm_bwd(qs, ks, proj3, gate_col, gate_row, h_f, g):
    b, t, _ = proj3.shape
    L = ML_CHUNK
    nc = t // L
    cur = lambda col: pl.BlockSpec((None, L, ML_WIDTH), lambda bi, c: (bi, nc - 1 - c, col))
    vc, oc = P_B // ML_WIDTH + 2, P_B // ML_WIDTH + 3
    return pl.pallas_call(
        functools.partial(_mlstm_bwd_kernel, L=L),
        grid=(b, nc),
        in_specs=[
            cur(0), cur(0), cur(vc),
            pl.BlockSpec((L, LANE), lambda bi, c: (bi * nc + nc - 1 - c, 0)),
            pl.BlockSpec((N_GATE, L), lambda bi, c: (0, bi * nc + nc - 1 - c)),
            cur(0), cur(oc),
            pl.BlockSpec((1, ML_WIDTH), lambda bi, c: (0, 0)),
        ],
        out_specs=cur(0),
        out_shape=jax.ShapeDtypeStruct((b, t, ML_WIDTH), BF16),
        scratch_shapes=_mlstm_scratch(),
        compiler_params=_cparams(("parallel", "arbitrary")),
        name="mlstm_bwd",
    )(qs, ks, proj3, gate_col, gate_row, h_f, proj3, g)


def _sg_kernel(u_ref, v_ref, g_ref, ws_ref, bst_ref, y_ref, *, chunks):
    for cidx in range(chunks):
        rs = slice(cidx * SG_CHUNK, (cidx + 1) * SG_CHUNK)
        zu = _gelu_tanh(u_ref[rs, :].astype(F32))
        zv = _gelu_tanh(v_ref[rs, :].astype(F32))
        vb = _rms(zv, g_ref[...]).astype(BF16)
        for gidx in range(SG_GROUPS):
            cs = slice(gidx * SG_GROUP_DIM, (gidx + 1) * SG_GROUP_DIM)
            mixed = _dot(ws_ref[gidx], vb[:, cs]) + bst_ref[:, gidx:gidx + 1]
            y_ref[rs, cs] = (zu[:, cs] * mixed).astype(BF16)


def _sg(proj, g, ws, bst, chunks=8):
    n = proj.shape[0]
    tm = chunks * SG_CHUNK
    row = lambda col: pl.BlockSpec((tm, SG_WIDTH), lambda i: (i, col))
    return pl.pallas_call(
        functools.partial(_sg_kernel, chunks=chunks),
        grid=(n // tm,),
        in_specs=[row(P_C // SG_WIDTH), row(P_C // SG_WIDTH + 1),
                  pl.BlockSpec((1, SG_WIDTH), lambda i: (0, 0)),
                  pl.BlockSpec((SG_GROUPS, SG_CHUNK, SG_CHUNK), lambda i: (0, 0, 0)),
                  pl.BlockSpec((SG_CHUNK, SG_GROUPS), lambda i: (0, 0))],
        out_specs=row(0),
        out_shape=jax.ShapeDtypeStruct((n, SG_WIDTH), BF16),
        compiler_params=_cparams(("parallel",)),
        name="sg",
    )(proj, proj, g, ws, bst)


def _mix_kernel(ya_ref, yb_ref, yc_ref, ga_ref, gb_ref, gc_ref, wa_ref, wb_ref, wc_ref,
                wo_ref, x_ref, g_ref, o_ref, m_scr, *, nj, tn):
    j = pl.program_id(1)

    def branch(y_ref, w_ref, gate_ref):
        return _sigmoid(gate_ref[...].astype(F32)) * _dot(y_ref[...], w_ref[j])

    merged = branch(ya_ref, wa_ref, ga_ref) + branch(yb_ref, wb_ref, gb_ref) + branch(yc_ref, wc_ref, gc_ref)
    m_scr[j] = merged.astype(BF16)

    @pl.when(j == nj - 1)
    def _():
        acc = _dot(m_scr[0], wo_ref[0:tn, :])
        for jj in range(1, nj):
            acc = acc + _dot(m_scr[jj], wo_ref[jj * tn:(jj + 1) * tn, :])
        o_ref[...] = x_ref[...] + _rms(acc, g_ref[...])


def _mix_out(ya, yb, yc, proj, wa, wb, wc, wo, l, xf, g, tm=512):
    n = xf.shape[0]
    _, nj, _, tn = wa.shape
    yspec = pl.BlockSpec((tm, NA_WIDTH), lambda i, j: (i, 0))
    gate = lambda k: pl.BlockSpec((tm, tn), lambda i, j: (i, (P_G + k * D_MODEL) // tn + j))
    wspec = pl.BlockSpec((None, nj, NA_WIDTH, tn), lambda i, j: (l, 0, 0, 0), pipeline_mode=pl.Buffered(1))
    return pl.pallas_call(
        functools.partial(_mix_kernel, nj=nj, tn=tn),
        grid=(n // tm, nj),
        in_specs=[yspec, yspec, yspec, gate(0), gate(1), gate(2), wspec, wspec, wspec,
                  pl.BlockSpec((None, D_MODEL, D_MODEL), lambda i, j: (l, 0, 0), pipeline_mode=pl.Buffered(1)),
                  pl.BlockSpec((tm, D_MODEL), lambda i, j: (i, 0)),
                  pl.BlockSpec((1, D_MODEL), lambda i, j: (0, 0))],
        out_specs=pl.BlockSpec((tm, D_MODEL), lambda i, j: (i, 0)),
        out_shape=jax.ShapeDtypeStruct((n, D_MODEL), F32),
        scratch_shapes=[pltpu.VMEM((nj, tm, tn), BF16)],
        compiler_params=_cparams(("parallel", "arbitrary"), vmem=BIG_VMEM_LIMIT),
        name="mix_out",
    )(ya, yb, yc, proj, proj, proj, wa, wb, wc, wo, xf, g)


def _mlp_kernel(x_ref, g1_ref, wu_ref, wd_ref, g2_ref, o_ref, h_scr, *, nj, tf, sub):
    j = pl.program_id(1)

    @pl.when(j == 0)
    def _():
        h_scr[...] = _rms(x_ref[...], g1_ref[...]).astype(BF16)
        o_ref[...] = jnp.zeros_like(o_ref)

    acc = o_ref[...]
    for s in range(tf // sub):
        a = jnp.maximum(_dot(h_scr[...], wu_ref[:, s * sub:(s + 1) * sub]), 0.0)
        acc = acc + _dot((a * a).astype(BF16), wd_ref[s * sub:(s + 1) * sub, :])
    o_ref[...] = acc

    @pl.when(j == nj - 1)
    def _():
        o_ref[...] = x_ref[...] + _rms(o_ref[...], g2_ref[...])


def _mlp(xf, g1, wu, wd, l, g2, tm=512, tf=2048, sub=1024):
    n = xf.shape[0]
    nj = D_FF // tf
    return pl.pallas_call(
        functools.partial(_mlp_kernel, nj=nj, tf=tf, sub=sub),
        grid=(n // tm, nj),
        in_specs=[pl.BlockSpec((tm, D_MODEL), lambda i, j: (i, 0)),
                  pl.BlockSpec((1, D_MODEL), lambda i, j: (0, 0)),
                  pl.BlockSpec((None, D_MODEL, tf), lambda i, j: (l, 0, j)),
                  pl.BlockSpec((None, tf, D_MODEL), lambda i, j: (l, j, 0)),
                  pl.BlockSpec((1, D_MODEL), lambda i, j: (0, 0))],
        out_specs=pl.BlockSpec((tm, D_MODEL), lambda i, j: (i, 0)),
        out_shape=jax.ShapeDtypeStruct((n, D_MODEL), F32),
        scratch_shapes=[pltpu.VMEM((tm, D_MODEL), BF16)],
        compiler_params=_cparams(("parallel", "arbitrary"), vmem=BIG_VMEM_LIMIT),
        name="mlp",
    )(xf, g1, wu, wd, g2)


def _regroup_kernel(xt_ref, xg_ref, o_ref, gt_ref):
    o_ref[...] = xt_ref[0].T.astype(BF16)

    @pl.when(pl.program_id(1) == 0)
    def _():
        gt_ref[...] = xg_ref[0].astype(BF16)


def _regroup_cast(w_in_t, tc=512):
    depth = w_in_t.shape[0]

    def src_row(l, c):
        r = c * tc
        return (l, pl.multiple_of(jnp.where(r < OFF_BG, r, r + N_GATE), N_GATE), 0)

    return pl.pallas_call(
        _regroup_kernel,
        grid=(depth, P_WIDTH // tc),
        in_specs=[pl.BlockSpec((pl.Element(1), pl.Element(tc), pl.Element(D_MODEL)), src_row),
                  pl.BlockSpec((pl.Element(1), pl.Element(N_GATE), pl.Element(D_MODEL)),
                               lambda l, c: (l, OFF_BG, 0))],
        out_specs=[pl.BlockSpec((None, D_MODEL, tc), lambda l, c: (l, 0, c)),
                   pl.BlockSpec((None, N_GATE, D_MODEL), lambda l, c: (l, 0, 0))],
        out_shape=[jax.ShapeDtypeStruct((depth, D_MODEL, P_WIDTH), BF16),
                   jax.ShapeDtypeStruct((depth, N_GATE, D_MODEL), BF16)],
        compiler_params=_cparams(("parallel", "arbitrary")),
        name="regroup_cast",
    )(w_in_t, w_in_t)


def _col_tiles(w, tn=MIX_TN):
    d, k, n = w.shape
    return w.astype(BF16).reshape(d, k, n // tn, tn).transpose(0, 2, 1, 3)


def _prep_shared(w_in, w_a, w_b, w_c, w_out, w_up, w_down):
    w_main, w_gate_t = _regroup_cast(jnp.swapaxes(w_in, 1, 2))
    return dict(w_main=w_main, w_gate_t=w_gate_t,
                w_a=_col_tiles(w_a), w_b=_col_tiles(w_b), w_c=_col_tiles(w_c),
                w_out=w_out.astype(BF16), w_up=w_up.astype(BF16), w_down=w_down.astype(BF16))


def _prep_layer(l, shared, pre_mix_g, post_mix_g, pre_mlp_g, post_mlp_g, b_gate, conv_w, na_rpb,
                ml_norm_g, sg_norm_g, w_s, b_s):
    w_gt = shared["w_gate_t"][l]
    p = dict(shared)
    p.update(
        l=l,
        pre_mix_g=pre_mix_g[l].reshape(1, D_MODEL), post_mix_g=post_mix_g[l].reshape(1, D_MODEL),
        pre_mlp_g=pre_mlp_g[l].reshape(1, D_MODEL), post_mlp_g=post_mlp_g[l].reshape(1, D_MODEL),
        w_gate=jnp.pad(w_gt.T, ((0, 0), (0, LANE - N_GATE))),
        bg_col=jnp.pad(b_gate[l].astype(F32), (0, LANE - N_GATE)).reshape(1, LANE),
        conv_w=conv_w[l].astype(F32), bias_tab=_na_bias_table(na_rpb[l]),
        ml_norm_g=ml_norm_g[l].reshape(1, ML_WIDTH), sg_norm_g=sg_norm_g[l].reshape(1, SG_WIDTH),
        w_s=w_s[l].astype(BF16), bst=b_s[l].astype(F32).T,
    )
    return p


def _trunk_layer(x, p):
    b, t, _ = x.shape
    n = b * t
    l = p["l"]
    xf = x.reshape(n, D_MODEL)
    proj, gate_col, gate_row = _in_proj(xf, p["pre_mix_g"], p["w_main"], l, p["w_gate"], p["bg_col"])
    proj3 = proj.reshape(b, t, P_WIDTH)
    y_a = _na(proj3, p["bias_tab"]).reshape(n, NA_WIDTH)
    h_f, qs, ks = _mlstm_fwd(proj3, gate_col, gate_row, p["conv_w"])
    y_b = _mlstm_bwd(qs, ks, proj3, gate_col, gate_row, h_f, p["ml_norm_g"]).reshape(n, ML_WIDTH)
    y_c = _sg(proj, p["sg_norm_g"], p["w_s"], p["bst"])
    x1 = _mix_out(y_a, y_b, y_c, proj, p["w_a"], p["w_b"], p["w_c"], p["w_out"], l, xf, p["post_mix_g"])
    x2 = _mlp(x1, p["pre_mlp_g"], p["w_up"], p["w_down"], l, p["post_mlp_g"])
    return x2.reshape(b, t, D_MODEL)


def kernel(x_prompt, x_sample, pre_mix_g, post_mix_g, pre_mlp_g, post_mlp_g, w_in, b_gate, conv_w,
           na_rpb, ml_norm_g, sg_norm_g, w_s, b_s, w_a, w_b, w_c, w_out, w_up, w_down):
    shared = _prep_shared(w_in, w_a, w_b, w_c, w_out, w_up, w_down)
    layers = [_prep_layer(l, shared, pre_mix_g, post_mix_g, pre_mlp_g, post_mlp_g, b_gate, conv_w,
                          na_rpb, ml_norm_g, sg_norm_g, w_s, b_s)
              for l in range(w_in.shape[0])]

    def run(x):
        for p in layers:
            x = _trunk_layer(x, p)
        return x

    return (run(x_prompt), run(x_sample))
```
